```python
import math, functools
import jax, jax.numpy as jnp
from jax import lax
import numpy as np

D_MODEL = 1024
BATCH = 8
SEQ = 4096
DEPTH = 2
DEC_BATCH = 128
DEC_SEQ = 8
PAST_LEN = 16384
PAGE_SIZE = 128

RWKV_HEADS = 4
RWKV_HEAD_DIM = 64
RWKV_W = RWKV_HEADS * RWKV_HEAD_DIM
DECAY_LORA = 64
ICLR_LORA = 64
GATE_LORA = 128
RWKV_IN = 3 * RWKV_W + DECAY_LORA + ICLR_LORA + GATE_LORA
GN_EPS = 64e-5

DIFF_HEADS = 4
DIFF_KV_HEADS = 2
DIFF_REP = DIFF_HEADS // DIFF_KV_HEADS
DIFF_DK = 64
DIFF_DV = 2 * DIFF_DK
DIFF_Q_W = DIFF_HEADS * 2 * DIFF_DK
DIFF_K_W = DIFF_KV_HEADS * 2 * DIFF_DK
DIFF_V_W = DIFF_KV_HEADS * DIFF_DV
DIFF_OUT_W = DIFF_HEADS * DIFF_DV
SUBLN_EPS = 1e-5

MLA_HEADS = 4
MLA_NOPE = 64
MLA_ROPE = 32
MLA_V = 64
MLA_Q_LORA = 192
MLA_KV_LORA = 128
MLA_IN = MLA_Q_LORA + MLA_KV_LORA + MLA_ROPE
MLA_OUT_W = MLA_HEADS * MLA_V
MLA_CACHE_W = MLA_KV_LORA + MLA_ROPE
ROPE_THETA = 10000.0

IN_W = RWKV_IN + DIFF_Q_W + DIFF_K_W + DIFF_V_W + MLA_IN
MIX_W = RWKV_W + DIFF_OUT_W + MLA_OUT_W

MEM_LEN = 256
MEM_HEADS = 4
MEM_HEAD_DIM = 64
MEM_W = MEM_HEADS * MEM_HEAD_DIM

D_FF = 2816
N_EXPERTS = 8
TOP_K = 2
MOE_D_FF = 3584

Q_BLOCK = 128
RMS_EPS = 1e-6

kernel_name = 'hybrid_rwkv7_diff_mla_step'


def rmsnorm(x, g, eps=RMS_EPS):
    xf = x.astype(jnp.float32)
    y = xf * lax.rsqrt(jnp.mean(xf * xf, axis=-1, keepdims=True) + eps)
    return (y * g.astype(jnp.float32)).astype(x.dtype)


def rope(x, pos):
    half = MLA_ROPE // 2
    inv = ROPE_THETA ** (-jnp.arange(half, dtype=jnp.float32) / half)
    ang = pos.astype(jnp.float32)[:, None] * inv[None, :]
    shp = (pos.shape[0],) + (1,) * (x.ndim - 3) + (half,)
    cos, sin = jnp.cos(ang).reshape(shp), jnp.sin(ang).reshape(shp)
    xf = x.astype(jnp.float32)
    x1, x2 = xf[..., :half], xf[..., half:]
    return jnp.concatenate([x1 * cos - x2 * sin, x1 * sin + x2 * cos], axis=-1).astype(x.dtype)


def attend(q, k, v, q_pos, k_pos, map_w, scale):
    s = jnp.einsum('bqhrmd,bkhmd->bhrmqk', q, k, preferred_element_type=jnp.float32) * scale
    mask = k_pos[None, :] <= q_pos[:, None]
    s = jnp.where(mask, s, -jnp.inf)
    pr = jax.nn.softmax(s, axis=-1)
    pr = jnp.einsum('bhrmqk,m->bhrqk', pr, map_w.astype(jnp.float32))
    return jnp.einsum('bhrqk,bkhd->bqhrd', pr.astype(v.dtype), v)


def prompt_attend(q, k, v, map_w, scale):
    B, T = q.shape[:2]
    nb = T // Q_BLOCK
    qb = jnp.moveaxis(q.reshape((B, nb, Q_BLOCK) + q.shape[2:]), 1, 0)
    k_pos = jnp.arange(T)

    def one_block(args):
        qi, i = args
        q_pos = i * Q_BLOCK + jnp.arange(Q_BLOCK)
        return attend(qi, k, v, q_pos, k_pos, map_w, scale)

    o = jnp.moveaxis(lax.map(one_block, (qb, jnp.arange(nb))), 0, 1)
    return o.reshape((B, T) + o.shape[3:])


def gather_diff(ck, cv, l, pt):
    return (ck[l, pt].reshape(-1, DIFF_KV_HEADS, 2, DIFF_DK),
            cv[l, pt].reshape(-1, DIFF_KV_HEADS, DIFF_DV))


def gather_mla(ckv, l, pt):
    kv = ckv[l, pt].reshape(-1, MLA_CACHE_W)
    return kv[:, None, None, :], kv[:, None, :MLA_KV_LORA]


def sample_attend(q, k_new, v_new, map_w, scale, gather_past, page_table):
    Tn = q.shape[1]
    past = page_table.shape[1] * PAGE_SIZE
    q_pos = past + jnp.arange(Tn)
    k_pos = jnp.arange(past + Tn)

    def one_seq(args):
        qb, kb, vb, pt = args
        kp, vp = gather_past(pt)
        kk = jnp.concatenate([kp.astype(kb.dtype), kb], axis=0)[None]
        vv = jnp.concatenate([vp.astype(vb.dtype), vb], axis=0)[None]
        return attend(qb[None], kk, vv, q_pos, k_pos, map_w, scale)[0]

    return lax.map(one_seq, (q, k_new, v_new, page_table))


def rwkv7_mix(p, l, proj, prev_row, S0):
    B, T, _ = proj.shape
    f32 = jnp.float32
    shifted = jnp.concatenate([prev_row[:, None].astype(proj.dtype), proj[:, :-1]], axis=1)
    xm = proj + (shifted - proj) * p['rwkv_mu'][l]
    c = [RWKV_W, 2 * RWKV_W, 3 * RWKV_W, 3 * RWKV_W + DECAY_LORA, 3 * RWKV_W + DECAY_LORA + ICLR_LORA]
    r, k, v, xw, xa, xg = jnp.split(xm, c, axis=-1)
    wraw = -jax.nn.softplus(-(p['rwkv_w0'][l] + jnp.tanh(xw) @ p['rwkv_w2'][l]).astype(f32)) - 0.5
    decay = jnp.exp(-jnp.exp(wraw))
    a = jax.nn.sigmoid((p['rwkv_a0'][l] + xa @ p['rwkv_a2'][l]).astype(f32))
    g = (jax.nn.sigmoid(xg) @ p['rwkv_g2'][l]).astype(f32)
    hs = lambda t: t.reshape(B, T, RWKV_HEADS, RWKV_HEAD_DIM)
    kk = hs(k.astype(f32) * p['rwkv_kk'][l].astype(f32))
    kk = kk / jnp.maximum(jnp.sqrt(jnp.sum(kk * kk, axis=-1, keepdims=True)), 1e-12)
    kf = hs(k.astype(f32) * (1.0 + (a - 1.0) * p['rwkv_ka'][l].astype(f32)))
    rf, vf, ah, wh = hs(r.astype(f32)), hs(v.astype(f32)), hs(a), hs(decay)

    def step(S, inp):
        r_t, w_t, k_t, v_t, kk_t, a_t = inp
        sa = jnp.einsum('bhvk,bhk->bhv', S, -kk_t)
        S = (S * w_t[:, :, None, :] + sa[..., None] * (kk_t * a_t)[:, :, None, :]
             + v_t[..., None] * k_t[:, :, None, :])
        return S, jnp.einsum('bhvk,bhk->bhv', S, r_t)

    tm = lambda t: jnp.moveaxis(t, 1, 0)
    S_T, y = lax.scan(step, S0.astype(f32), (tm(rf), tm(wh), tm(kf), tm(vf), tm(kk), tm(ah)))
    y = jnp.moveaxis(y, 0, 1)
    mu = jnp.mean(y, axis=-1, keepdims=True)
    var = jnp.mean(jnp.square(y - mu), axis=-1, keepdims=True)
    y = ((y - mu) * lax.rsqrt(var + GN_EPS)).reshape(B, T, RWKV_W)
    y = y * p['rwkv_lnw'][l].astype(f32) + p['rwkv_lnb'][l].astype(f32)
    bonus = jnp.sum(rf * kf * p['rwkv_rk'][l].astype(f32), axis=-1, keepdims=True) * vf
    y = (y + bonus.reshape(B, T, RWKV_W)) * g
    return y.astype(proj.dtype), S_T.astype(S0.dtype), proj[:, -1]


def diff_lambda(lp, l):
    lam_init = 0.8 - 0.6 * math.exp(-0.3 * l)
    lp = lp.astype(jnp.float32)
    lam = jnp.exp(jnp.sum(lp[0] * lp[1])) - jnp.exp(jnp.sum(lp[2] * lp[3])) + lam_init
    return jnp.stack([jnp.ones((), jnp.float32), -lam]), lam_init


def token_mixers(p, l, n, pos, shift0, S0, attn_diff, attn_mla):
    B, T, _ = n.shape
    proj = n @ p['w_in'][l]
    c1 = RWKV_IN
    c2 = c1 + DIFF_Q_W
    c3 = c2 + DIFF_K_W
    c4 = c3 + DIFF_V_W
    c5 = c4 + MLA_Q_LORA
    c6 = c5 + MLA_KV_LORA
    x_rw, q_d, k_d, v_d, c_q, c_kv, k_r = jnp.split(proj, [c1, c2, c3, c4, c5, c6], axis=-1)
    y_rw, S_T, last = rwkv7_mix(p, l, x_rw, shift0, S0)
    q_d = q_d.reshape(B, T, DIFF_KV_HEADS, DIFF_REP, 2, DIFF_DK)
    k_d = k_d.reshape(B, T, DIFF_KV_HEADS, 2, DIFF_DK)
    v_d = v_d.reshape(B, T, DIFF_KV_HEADS, DIFF_DV)
    map_w, lam_init = diff_lambda(p['diff_lambda'][l], l)
    o_d = attn_diff(q_d, k_d, v_d, map_w, DIFF_DK ** -0.5)
    y_d = (rmsnorm(o_d, p['diff_subln'][l], SUBLN_EPS) * (1.0 - lam_init)).reshape(B, T, DIFF_OUT_W)
    q = jnp.einsum('btc,chd->bthd', rmsnorm(c_q, p['mla_qnorm'][l]), p['mla_wuq'][l])
    q_nope, q_rope = q[..., :MLA_NOPE], rope(q[..., MLA_NOPE:], pos)
    q_lat = jnp.einsum('bthn,chn->bthc', q_nope, p['mla_wuk'][l])
    q_m = jnp.concatenate([q_lat, q_rope], axis=-1)[:, :, None, :, None, :]
    kv = jnp.concatenate([rmsnorm(c_kv, p['mla_kvnorm'][l]), rope(k_r, pos)], axis=-1)
    o_m = attn_mla(q_m, kv[:, :, None, None, :], kv[:, :, None, :MLA_KV_LORA],
                   jnp.ones((1,), jnp.float32), (MLA_NOPE + MLA_ROPE) ** -0.5)
    y_m = jnp.einsum('bthc,chv->bthv', o_m[:, :, 0], p['mla_wuv'][l]).reshape(B, T, MLA_OUT_W)
    mix = jnp.concatenate([y_rw, y_d, y_m], axis=-1) @ p['w_out'][l]
    return mix, S_T, last, k_d, v_d, kv


def memory_kv(p, l, mem):
    B, M, _ = mem.shape
    m = rmsnorm(mem, p['norm_mem'][l])
    k = (m @ p['w_xk'][l]).reshape(B, M, MEM_HEADS, MEM_HEAD_DIM)
    v = (m @ p['w_xv'][l]).reshape(B, M, MEM_HEADS, MEM_HEAD_DIM)
    return k, v


def cross_attend(p, l, h, mk, mv):
    B, T, _ = h.shape
    q = (rmsnorm(h, p['norm_cross'][l]) @ p['w_xq'][l]).reshape(B, T, MEM_HEADS, MEM_HEAD_DIM)
    s = jnp.einsum('bqhd,bmhd->bhqm', q, mk, preferred_element_type=jnp.float32) * MEM_HEAD_DIM ** -0.5
    pr = jax.nn.softmax(s, axis=-1)
    o = jnp.einsum('bhqm,bmhd->bqhd', pr.astype(mv.dtype), mv).reshape(B, T, MEM_W)
    return o @ p['w_xo'][l]


def channel_mix(p, l, h):
    n = rmsnorm(h, p['norm_ffn'][l])
    i = l // 2
    if l % 2 == 0:
        return (jax.nn.silu(n @ p['ffn_wg'][i]) * (n @ p['ffn_wu'][i])) @ p['ffn_wd'][i]
    logits = jnp.einsum('btd,de->bte', n, p['moe_router'][i], preferred_element_type=jnp.float32)
    top_v, top_i = lax.top_k(logits, TOP_K)
    gates = jax.nn.softmax(top_v, axis=-1)
    comb = jnp.sum(jax.nn.one_hot(top_i, N_EXPERTS, dtype=jnp.float32) * gates[..., None], axis=-2)
    out = jnp.zeros_like(n)
    for e in range(N_EXPERTS):
        he = jax.nn.silu(n @ p['moe_wg'][i, e]) * (n @ p['moe_wu'][i, e])
        out = out + comb[..., e:e + 1].astype(n.dtype) * (he @ p['moe_wd'][i, e])
    return out


def layer(p, l, x, pos, shift0, S0, attn_diff, attn_mla, mk, mv):
    n = rmsnorm(x, p['norm_mix'][l])
    mix, S_T, last, k_d, v_d, kv = token_mixers(p, l, n, pos, shift0, S0, attn_diff, attn_mla)
    h = x + mix
    h = h + cross_attend(p, l, h, mk, mv)
    h = h + channel_mix(p, l, h)
    return h, S_T, last, k_d, v_d, kv


def setup_inputs(seed: int = 0) -> dict:
    key = jax.random.key(seed)
    keys = iter(jax.random.split(key, 64))
    f32 = jnp.float32

    def nrm(shape, scale=1.0):
        return jax.random.normal(next(keys), shape, f32) * scale

    def gain(shape):
        return 1.0 + nrm(shape, 0.02)

    def unif(shape, lo, hi):
        return jax.random.uniform(next(keys), shape, f32, lo, hi)

    n_pages = PAST_LEN // PAGE_SIZE
    n_used = DEC_BATCH * n_pages
    n_pool = n_used + (n_used + 3) // 4
    n_dense = (DEPTH + 1) // 2
    n_moe = DEPTH // 2
    page_table = jax.random.permutation(next(keys), n_pool)[:n_used].reshape(DEC_BATCH, n_pages).astype(jnp.int32)
    return {
        'x_prompt': nrm((BATCH, SEQ, D_MODEL)),
        'x_sample': nrm((DEC_BATCH, DEC_SEQ, D_MODEL)),
        'state_rwkv': nrm((DEPTH, DEC_BATCH, RWKV_HEADS, RWKV_HEAD_DIM, RWKV_HEAD_DIM), 0.3),
        'state_shift': nrm((DEPTH, DEC_BATCH, RWKV_IN)),
        'cache_diff_k': nrm((DEPTH, n_pool, PAGE_SIZE, DIFF_KV_HEADS, 2, DIFF_DK)),
        'cache_diff_v': nrm((DEPTH, n_pool, PAGE_SIZE, DIFF_KV_HEADS, DIFF_DV)),
        'cache_mla_kv': nrm((DEPTH, n_pool, PAGE_SIZE, MLA_CACHE_W)),
        'cache_mem_k': nrm((DEPTH, DEC_BATCH, MEM_LEN, MEM_HEADS, MEM_HEAD_DIM)),
        'cache_mem_v': nrm((DEPTH, DEC_BATCH, MEM_LEN, MEM_HEADS, MEM_HEAD_DIM)),
        'page_table': page_table,
        'mem_prompt': nrm((BATCH, MEM_LEN, D_MODEL)),
        'norm_mix': gain((DEPTH, D_MODEL)),
        'w_in': nrm((DEPTH, D_MODEL, IN_W), D_MODEL ** -0.5),
        'rwkv_mu': unif((DEPTH, RWKV_IN), 0.0, 1.0),
        'rwkv_w0': unif((DEPTH, RWKV_W), -6.0, 0.0),
        'rwkv_w2': nrm((DEPTH, DECAY_LORA, RWKV_W), 0.1 * DECAY_LORA ** -0.5),
        'rwkv_a0': nrm((DEPTH, RWKV_W), 0.1),
        'rwkv_a2': nrm((DEPTH, ICLR_LORA, RWKV_W), 0.1 * ICLR_LORA ** -0.5),
        'rwkv_g2': nrm((DEPTH, GATE_LORA, RWKV_W), GATE_LORA ** -0.5),
        'rwkv_kk': 0.85 + nrm((DEPTH, RWKV_W), 0.02),
        'rwkv_ka': gain((DEPTH, RWKV_W)),
        'rwkv_rk': nrm((DEPTH, RWKV_HEADS, RWKV_HEAD_DIM), 0.1),
        'rwkv_lnw': gain((DEPTH, RWKV_W)),
        'rwkv_lnb': nrm((DEPTH, RWKV_W), 0.02),
        'diff_lambda': nrm((DEPTH, 4, DIFF_DK), 0.1),
        'diff_subln': gain((DEPTH, DIFF_DV)),
        'mla_qnorm': gain((DEPTH, MLA_Q_LORA)),
        'mla_kvnorm': gain((DEPTH, MLA_KV_LORA)),
        'mla_wuq': nrm((DEPTH, MLA_Q_LORA, MLA_HEADS, MLA_NOPE + MLA_ROPE), MLA_Q_LORA ** -0.5),
        'mla_wuk': nrm((DEPTH, MLA_KV_LORA, MLA_HEADS, MLA_NOPE), MLA_KV_LORA ** -0.5),
        'mla_wuv': nrm((DEPTH, MLA_KV_LORA, MLA_HEADS, MLA_V), MLA_KV_LORA ** -0.5),
        'w_out': nrm((DEPTH, MIX_W, D_MODEL), MIX_W ** -0.5),
        'norm_cross': gain((DEPTH, D_MODEL)),
        'norm_mem': gain((DEPTH, D_MODEL)),
        'w_xq': nrm((DEPTH, D_MODEL, MEM_W), D_MODEL ** -0.5),
        'w_xk': nrm((DEPTH, D_MODEL, MEM_W), D_MODEL ** -0.5),
        'w_xv': nrm((DEPTH, D_MODEL, MEM_W), D_MODEL ** -0.5),
        'w_xo': nrm((DEPTH, MEM_W, D_MODEL), MEM_W ** -0.5),
        'norm_ffn': gain((DEPTH, D_MODEL)),
        'ffn_wg': nrm((n_dense, D_MODEL, D_FF), D_MODEL ** -0.5),
        'ffn_wu': nrm((n_dense, D_MODEL, D_FF), D_MODEL ** -0.5),
        'ffn_wd': nrm((n_dense, D_FF, D_MODEL), D_FF ** -0.5),
        'moe_router': nrm((n_moe, D_MODEL, N_EXPERTS), D_MODEL ** -0.5),
        'moe_wg': nrm((n_moe, N_EXPERTS, D_MODEL, MOE_D_FF), D_MODEL ** -0.5),
        'moe_wu': nrm((n_moe, N_EXPERTS, D_MODEL, MOE_D_FF), D_MODEL ** -0.5),
        'moe_wd': nrm((n_moe, N_EXPERTS, MOE_D_FF, D_MODEL), MOE_D_FF ** -0.5),
        'final_norm': gain((D_MODEL,)),
    }


def reference(x_prompt, x_sample, state_rwkv, state_shift, cache_diff_k, cache_diff_v, cache_mla_kv,
              cache_mem_k, cache_mem_v, page_table, mem_prompt, norm_mix, w_in, rwkv_mu, rwkv_w0, rwkv_w2,
              rwkv_a0, rwkv_a2, rwkv_g2, rwkv_kk, rwkv_ka, rwkv_rk, rwkv_lnw, rwkv_lnb, diff_lambda, diff_subln,
              mla_qnorm, mla_kvnorm, mla_wuq, mla_wuk, mla_wuv, w_out, norm_cross, norm_mem, w_xq, w_xk, w_xv,
              w_xo, norm_ffn, ffn_wg, ffn_wu, ffn_wd, moe_router, moe_wg, moe_wu, moe_wd, final_norm):
    p = dict(norm_mix=norm_mix, w_in=w_in, rwkv_mu=rwkv_mu, rwkv_w0=rwkv_w0, rwkv_w2=rwkv_w2,
             rwkv_a0=rwkv_a0, rwkv_a2=rwkv_a2, rwkv_g2=rwkv_g2, rwkv_kk=rwkv_kk, rwkv_ka=rwkv_ka,
             rwkv_rk=rwkv_rk, rwkv_lnw=rwkv_lnw, rwkv_lnb=rwkv_lnb, diff_lambda=diff_lambda,
             diff_subln=diff_subln, mla_qnorm=mla_qnorm, mla_kvnorm=mla_kvnorm, mla_wuq=mla_wuq,
             mla_wuk=mla_wuk, mla_wuv=mla_wuv, w_out=w_out, norm_cross=norm_cross, norm_mem=norm_mem,
             w_xq=w_xq, w_xk=w_xk, w_xv=w_xv, w_xo=w_xo, norm_ffn=norm_ffn, ffn_wg=ffn_wg, ffn_wu=ffn_wu,
             ffn_wd=ffn_wd, moe_router=moe_router, moe_wg=moe_wg, moe_wu=moe_wu, moe_wd=moe_wd)
    Bp, T = x_prompt.shape[:2]
    pos_p = jnp.arange(T)
    pos_s = PAST_LEN + jnp.arange(x_sample.shape[1])
    hp, hs = x_prompt, x_sample
    p_S, p_sh, p_dk, p_dv, p_kv, p_mk, p_mv = [], [], [], [], [], [], []
    s_S, s_sh, s_dk, s_dv, s_kv = [], [], [], [], []
    for l in range(DEPTH):
        mk, mv = memory_kv(p, l, mem_prompt)
        S0 = jnp.zeros((Bp, RWKV_HEADS, RWKV_HEAD_DIM, RWKV_HEAD_DIM), jnp.float32)
        sh0 = jnp.zeros((Bp, RWKV_IN), x_prompt.dtype)
        hp, S_T, last, k_d, v_d, kv = layer(p, l, hp, pos_p, sh0, S0, prompt_attend, prompt_attend, mk, mv)
        p_S.append(S_T); p_sh.append(last); p_dk.append(k_d); p_dv.append(v_d); p_kv.append(kv)
        p_mk.append(mk); p_mv.append(mv)
        attn_d = functools.partial(sample_attend, gather_past=functools.partial(gather_diff, cache_diff_k, cache_diff_v, l), page_table=page_table)
        attn_m = functools.partial(sample_attend, gather_past=functools.partial(gather_mla, cache_mla_kv, l), page_table=page_table)
        hs, S_T, last, k_d, v_d, kv = layer(p, l, hs, pos_s, state_shift[l], state_rwkv[l], attn_d, attn_m, cache_mem_k[l], cache_mem_v[l])
        s_S.append(S_T); s_sh.append(last); s_dk.append(k_d); s_dv.append(v_d); s_kv.append(kv)
    y_prompt = rmsnorm(hp, final_norm)
    y_sample = rmsnorm(hs, final_norm)
    return (y_prompt, y_sample, jnp.stack(p_S), jnp.stack(p_sh), jnp.stack(p_dk), jnp.stack(p_dv), jnp.stack(p_kv), jnp.stack(p_mk), jnp.stack(p_mv), jnp.stack(s_S), jnp.stack(s_sh), jnp.stack(s_dk), jnp.stack(s_dv), jnp.stack(s_kv))
```

```python
import functools
import math

import jax
import jax.numpy as jnp
from jax import lax
from jax.experimental import pallas as pl
from jax.experimental.pallas import tpu as pltpu

F32 = jnp.float32
BF16 = jnp.bfloat16

D_MODEL = 1024
PAGE_SIZE = 128
RWKV_HEADS = 4
RWKV_N = 64
RWKV_W = 256
DECAY_LORA = 64
ICLR_LORA = 64
GATE_LORA = 128
RWKV_IN = 1024
GN_EPS = 64e-5
DIFF_DK = 64
DIFF_DV = 128
DIFF_Q_W = 512
DIFF_K_W = 256
DIFF_V_W = 256
SUBLN_EPS = 1e-5
MLA_HEADS = 4
MLA_NOPE = 64
MLA_ROPE = 32
MLA_V = 64
MLA_Q_LORA = 192
MLA_KV_LORA = 128
MLA_CACHE_W = 160
ROPE_THETA = 10000.0
MEM_HEADS = 4
MEM_HEAD_DIM = 64
MEM_W = 256
N_EXPERTS = 8
RMS_EPS = 1e-6
NEG_BIG = -1e30

VMEM_LIMIT_BYTES = 56 * 1024 * 1024
MOE_CHUNK = 128
PAGES_PER_STEP = 8


def _cparams(*sem):
    return pltpu.CompilerParams(dimension_semantics=sem, vmem_limit_bytes=VMEM_LIMIT_BYTES)


def _tile(n, pref):
    t = pref
    while t > 8 and n % t:
        t //= 2
    assert n % t == 0, (n, pref)
    return t


def _dot(a, b):
    return jnp.dot(a, b, preferred_element_type=F32)


def _dot_nt(a, b):
    return lax.dot_general(a, b, (((1,), (1,)), ((), ())), preferred_element_type=F32)


def _dot_hi(a, b):
    return jnp.dot(a, b, preferred_element_type=F32, precision=lax.Precision.HIGHEST)


def _rms(x, g, eps):
    return x * lax.rsqrt(jnp.mean(x * x, axis=-1, keepdims=True) + eps) * g


def _full(shape):
    nd = len(shape)
    return pl.BlockSpec(shape, lambda *a, _nd=nd: (0,) * _nd)


def _mixer_in_kernel(x_ref, g_ref, wrw, wq, wk, wv, wcq, wckv, wkr, wkrr, cos_ref, sin_ref, gq_ref, gkv_ref,
                     wqn, wqr, wqrr, wukt,
                     xrw_o, qd_o, ktok_o, kh_o, vtok_o, vbf_o, kv_o, kvbf_o, qm_o, *, mla_scale):
    n = _rms(x_ref[...], g_ref[...], RMS_EPS).astype(BF16)
    xrw_o[...] = _dot(n, wrw[...])
    q = _dot(n, wq[...]).astype(BF16)
    for j in range(8):
        qd_o[j] = q[:, j * DIFF_DK:(j + 1) * DIFF_DK]
    k = _dot(n, wk[...])
    ktok_o[...] = k
    kb = k.astype(BF16)
    for j in range(4):
        kh_o[j] = kb[:, j * DIFF_DK:(j + 1) * DIFF_DK]
    v = _dot(n, wv[...])
    vtok_o[...] = v
    vbf_o[...] = v.astype(BF16)
    cos = cos_ref[...]
    sin = sin_ref[...]
    ckvn = _rms(_dot(n, wckv[...]), gkv_ref[...], RMS_EPS)
    kr = _dot(n, wkr[...]) * cos + _dot(n, wkrr[...]) * sin
    kv_o[:, :MLA_KV_LORA] = ckvn
    kv_o[:, MLA_KV_LORA:] = kr
    kvbf_o[:, :MLA_KV_LORA] = ckvn.astype(BF16)
    kvbf_o[:, MLA_KV_LORA:] = kr.astype(BF16)
    cqn = _rms(_dot(n, wcq[...]), gq_ref[...], RMS_EPS).astype(BF16)
    for h in range(MLA_HEADS):
        qn = _dot(cqn, wqn[h]).astype(BF16)
        qlat = _dot(qn, wukt[h])
        qr = _dot(cqn, wqr[h]) * cos + _dot(cqn, wqrr[h]) * sin
        qm_o[h, :, :MLA_KV_LORA] = (qlat * mla_scale).astype(BF16)
        qm_o[h, :, MLA_KV_LORA:] = (qr * mla_scale).astype(BF16)


def _mixer_in(x, lw, cos, sin):
    n = x.shape[0]
    tm = _tile(n, 512)
    row = lambda w: pl.BlockSpec((tm, w), lambda i: (i, 0))
    slab = lambda s, w: pl.BlockSpec((s, tm, w), lambda i: (0, i, 0))
    ws = [lw['w_rw'], lw['w_q'], lw['w_k'], lw['w_v'], lw['w_cq'], lw['w_ckv'], lw['w_kr'], lw['w_krr']]
    tail = [lw['g_q'], lw['g_kv'], lw['wq_nope'], lw['wq_rope'], lw['wq_rope_rot'], lw['wuk_t']]
    in_specs = ([row(D_MODEL), _full((1, D_MODEL))] + [_full(w.shape) for w in ws]
                + [row(MLA_ROPE), row(MLA_ROPE)] + [_full(w.shape) for w in tail])
    out_shape = [
        jax.ShapeDtypeStruct((n, RWKV_IN), F32),
        jax.ShapeDtypeStruct((8, n, DIFF_DK), BF16),
        jax.ShapeDtypeStruct((n, DIFF_K_W), F32),
        jax.ShapeDtypeStruct((4, n, DIFF_DK), BF16),
        jax.ShapeDtypeStruct((n, DIFF_V_W), F32),
        jax.ShapeDtypeStruct((n, DIFF_V_W), BF16),
        jax.ShapeDtypeStruct((n, MLA_CACHE_W), F32),
        jax.ShapeDtypeStruct((n, MLA_CACHE_W), BF16),
        jax.ShapeDtypeStruct((MLA_HEADS, n, MLA_CACHE_W), BF16),
    ]
    out_specs = [row(RWKV_IN), slab(8, DIFF_DK), row(DIFF_K_W), slab(4, DIFF_DK), row(DIFF_V_W), row(DIFF_V_W),
                 row(MLA_CACHE_W), row(MLA_CACHE_W), slab(MLA_HEADS, MLA_CACHE_W)]
    return pl.pallas_call(
        functools.partial(_mixer_in_kernel, mla_scale=(MLA_NOPE + MLA_ROPE) ** -0.5),
        grid=(n // tm,), in_specs=in_specs, out_specs=out_specs, out_shape=out_shape,
        compiler_params=_cparams("parallel"), name="mixer_in",
    )(x, lw['g_mix'], *ws, cos, sin, *tail)


def _rwkv_prep_kernel(x_ref, prev_ref, s0_ref, mu_ref, w0_ref, a0_ref, kkw_ref, ka_ref, rk_ref, w2_ref, a2_ref,
                      g2_ref, bones_ref, r_o, w_o, kf_o, v_o, nkk_o, kka_o, g_o, bonus_o):
    i = pl.program_id(1)
    x = x_ref[...]
    first = jnp.where(i == 0, s0_ref[0], prev_ref[7:8, :])
    rowid = lax.broadcasted_iota(jnp.int32, x.shape, 0)
    shifted = jnp.where(rowid == 0, first, pltpu.roll(x, 1, 0))
    xm = x + (shifted - x) * mu_ref[...]
    r = xm[:, 0:RWKV_W]
    k = xm[:, RWKV_W:2 * RWKV_W]
    v = xm[:, 2 * RWKV_W:3 * RWKV_W]
    c = 3 * RWKV_W
    xw = xm[:, c:c + DECAY_LORA]
    xa = xm[:, c + DECAY_LORA:c + DECAY_LORA + ICLR_LORA]
    xg = xm[:, c + DECAY_LORA + ICLR_LORA:]
    z = -(w0_ref[...] + _dot(jnp.tanh(xw).astype(BF16), w2_ref[...]))
    softplus = jnp.maximum(z, 0.0) + jnp.log(1.0 + jnp.exp(-jnp.abs(z)))
    decay = jnp.exp(-jnp.exp(-softplus - 0.5))
    a = jax.nn.sigmoid(a0_ref[...] + _dot(xa.astype(BF16), a2_ref[...]))
    g = _dot(jax.nn.sigmoid(xg).astype(BF16), g2_ref[...])
    kkr = k * kkw_ref[...]
    kk = kkr / jnp.maximum(jnp.sqrt(_dot_hi(kkr * kkr, bones_ref[...])), 1e-12)
    kf = k * (1.0 + (a - 1.0) * ka_ref[...])
    bonus = _dot_hi(r * kf * rk_ref[...], bones_ref[...]) * v
    g_o[...] = g
    bonus_o[...] = bonus
    nkk = -kk
    kka = kk * a
    for h in range(RWKV_HEADS):
        sl = slice(h * RWKV_N, (h + 1) * RWKV_N)
        r_o[0, h] = r[:, sl]
        w_o[0, h] = decay[:, sl]
        kf_o[0, h] = kf[:, sl]
        v_o[0, h] = v[:, sl]
        nkk_o[0, h] = nkk[:, sl]
        kka_o[0, h] = kka[:, sl]


def _rwkv_prep(xrw, row0, nb, t, shift0, lw):
    tm = _tile(t, 512)
    nt = t // tm
    blk0 = row0 // tm
    blk0_8 = row0 // 8
    in_specs = [
        pl.BlockSpec((tm, RWKV_IN), lambda b, i: (blk0 + b * nt + i, 0)),
        pl.BlockSpec((8, RWKV_IN), lambda b, i: (jnp.maximum(blk0_8 + (b * t + i * tm) // 8 - 1, 0), 0)),
        pl.BlockSpec((1, 1, RWKV_IN), lambda b, i: (b, 0, 0)),
    ]
    params = [lw['rw_mu'], lw['rw_w0'], lw['rw_a0'], lw['rw_kk'], lw['rw_ka'], lw['rw_rk'], lw['rw_w2'],
              lw['rw_a2'], lw['rw_g2'], lw['bones']]
    in_specs += [_full(p.shape) for p in params]
    hm = pl.BlockSpec((1, RWKV_HEADS, tm, RWKV_N), lambda b, i: (b, 0, i, 0))
    tokm = pl.BlockSpec((tm, RWKV_W), lambda b, i: (b * nt + i, 0))
    hm_shape = jax.ShapeDtypeStruct((nb, RWKV_HEADS, t, RWKV_N), F32)
    tok_shape = jax.ShapeDtypeStruct((nb * t, RWKV_W), F32)
    return pl.pallas_call(
        _rwkv_prep_kernel, grid=(nb, nt), in_specs=in_specs,
        out_specs=[hm] * 6 + [tokm] * 2, out_shape=[hm_shape] * 6 + [tok_shape] * 2,
        compiler_params=_cparams("parallel", "arbitrary"), name="rwkv_prep",
    )(xrw, xrw, shift0.reshape(nb, 1, RWKV_IN), *params)


def _rwkv_scan_kernel(r_ref, w_ref, kf_ref, v_ref, nkk_ref, kka_ref, s0_ref, g_ref, bonus_ref, lnw_ref, lnb_ref,
                      y_o, st_o, s_sc, y_sc, *, tb):
    i = pl.program_id(1)

    @pl.when(i == 0)
    def _():
        s_sc[...] = s0_ref[0]

    eye = (lax.broadcasted_iota(jnp.int32, (RWKV_N, RWKV_N), 0)
           == lax.broadcasted_iota(jnp.int32, (RWKV_N, RWKV_N), 1)).astype(F32)

    def step(t, states):
        out = []
        for h in range(RWKV_HEADS):
            s = states[h]
            row = lambda ref: ref[0, h, pl.ds(t, 1), :]
            sa = jnp.sum(s * row(nkk_ref), axis=-1, keepdims=True)
            v_col = jnp.sum(eye * row(v_ref), axis=-1, keepdims=True)
            s = s * row(w_ref) + sa * row(kka_ref) + v_col * row(kf_ref)
            y_col = jnp.sum(s * row(r_ref), axis=-1, keepdims=True)
            y_sc[h, pl.ds(t, 1), :] = jnp.sum(eye * y_col, axis=0, keepdims=True)
            out.append(s)
        return tuple(out)

    states = lax.fori_loop(0, tb, step, tuple(s_sc[h] for h in range(RWKV_HEADS)))
    for h in range(RWKV_HEADS):
        s_sc[h] = states[h]
        sl = slice(h * RWKV_N, (h + 1) * RWKV_N)
        y = y_sc[h]
        mu = jnp.mean(y, axis=-1, keepdims=True)
        yc = y - mu
        var = jnp.mean(yc * yc, axis=-1, keepdims=True)
        yn = yc * lax.rsqrt(var + GN_EPS) * lnw_ref[:, sl] + lnb_ref[:, sl]
        y_o[:, sl] = (yn + bonus_ref[:, sl]) * g_ref[:, sl]

    @pl.when(i == pl.num_programs(1) - 1)
    def _():
        st_o[0] = s_sc[...]


def _rwkv_scan(prep, s0, nb, t, lw):
    r, w, kf, v, nkk, kka, g, bonus = prep
    tb = _tile(t, 128)
    nt = t // tb
    hm = pl.BlockSpec((1, RWKV_HEADS, tb, RWKV_N), lambda b, i: (b, 0, i, 0))
    st = pl.BlockSpec((1, RWKV_HEADS, RWKV_N, RWKV_N), lambda b, i: (b, 0, 0, 0))
    tokm = pl.BlockSpec((tb, RWKV_W), lambda b, i: (b * nt + i, 0))
    return pl.pallas_call(
        functools.partial(_rwkv_scan_kernel, tb=tb), grid=(nb, nt),
        in_specs=[hm] * 6 + [st, tokm, tokm, _full((1, RWKV_W)), _full((1, RWKV_W))],
        out_specs=[tokm, st],
        out_shape=[jax.ShapeDtypeStruct((nb * t, RWKV_W), F32),
                   jax.ShapeDtypeStruct((nb, RWKV_HEADS, RWKV_N, RWKV_N), F32)],
        scratch_shapes=[pltpu.VMEM((RWKV_HEADS, RWKV_N, RWKV_N), F32), pltpu.VMEM((RWKV_HEADS, tb, RWKV_N), F32)],
        compiler_params=_cparams("parallel", "arbitrary"), name="rwkv_scan",
    )(r, w, kf, v, nkk, kka, s0, g, bonus, lw['rw_lnw'], lw['rw_lnb'])


def _diff_lambda(lam_ref, lam_init):
    lp = lam_ref[...]
    s1 = jnp.sum(lp[0:1] * lp[1:2], axis=-1, keepdims=True)
    s2 = jnp.sum(lp[2:3] * lp[3:4], axis=-1, keepdims=True)
    return jnp.exp(s1) - jnp.exp(s2) + lam_init


def _flash_kernel(qi_tab, ki_tab, *refs, mode, tq, lam_init):
    if mode == 'diff':
        q_ref, k_ref, v_ref, lam_ref, g_ref, o_ref, m_sc, l_sc, acc_sc = refs
        pid = pl.program_id(2)
        n_maps = 2
    else:
        q_ref, kv_ref, wuv_ref, o_ref, m_sc, l_sc, acc_sc = refs
        pid = pl.program_id(1)
        n_maps = 1
    qi = qi_tab[pid]
    ki = ki_tab[pid]
    rows = m_sc.shape[1]

    @pl.when(ki == 0)
    def _():
        m_sc[...] = jnp.full(m_sc.shape, NEG_BIG, F32)
        l_sc[...] = jnp.zeros(l_sc.shape, F32)
        acc_sc[...] = jnp.zeros(acc_sc.shape, F32)

    def update(masked):
        for m in range(n_maps):
            if mode == 'diff':
                q = q_ref[2 * m:2 * m + 2].reshape(rows, DIFF_DK)
                k = k_ref[m]
                v = v_ref[...]
            else:
                q = q_ref[...].reshape(rows, MLA_CACHE_W)
                k = kv_ref[...]
                v = k[:, :MLA_KV_LORA]
            s = _dot_nt(q, k)
            if masked:
                qpos = lax.broadcasted_iota(jnp.int32, s.shape, 0) % tq
                kpos = lax.broadcasted_iota(jnp.int32, s.shape, 1)
                s = jnp.where(kpos <= qpos, s, NEG_BIG)
            m_prev = m_sc[m]
            m_new = jnp.maximum(m_prev, jnp.max(s, axis=-1, keepdims=True))
            alpha = jnp.exp(m_prev - m_new)
            p = jnp.exp(s - m_new)
            l_sc[m] = alpha * l_sc[m] + jnp.sum(p, axis=-1, keepdims=True)
            acc_sc[m] = alpha * acc_sc[m] + _dot(p.astype(BF16), v)
            m_sc[m] = m_new

    @pl.when(ki < qi)
    def _():
        update(False)

    @pl.when(ki == qi)
    def _():
        update(True)
        if mode == 'diff':
            lam = _diff_lambda(lam_ref, lam_init)
            o = acc_sc[0] / l_sc[0] - lam * (acc_sc[1] / l_sc[1])
            y = _rms(o, g_ref[...], SUBLN_EPS) * (1.0 - lam_init)
            for r in range(2):
                o_ref[:, r * DIFF_DV:(r + 1) * DIFF_DV] = y[r * tq:(r + 1) * tq]
        else:
            o = (acc_sc[0] / l_sc[0]).astype(BF16)
            for h in range(MLA_HEADS):
                o_ref[:, h * MLA_V:(h + 1) * MLA_V] = _dot(o[h * tq:(h + 1) * tq], wuv_ref[h])


def _pair_tables(nq):
    qi = [q for q in range(nq) for _ in range(q + 1)]
    ki = [k for q in range(nq) for k in range(q + 1)]
    return jnp.asarray(qi, jnp.int32), jnp.asarray(ki, jnp.int32)


def _prompt_diff_attn(qd, kh, vbf, nb, t, lw, lam_init):
    tq = _tile(t, 512)
    nq = t // tq
    qi_tab, ki_tab = _pair_tables(nq)
    rows = 2 * tq
    grid_spec = pltpu.PrefetchScalarGridSpec(
        num_scalar_prefetch=2, grid=(nb, 2, len(qi_tab)),
        in_specs=[
            pl.BlockSpec((4, tq, DIFF_DK), lambda b, h, p, qt, kt: (h, b * nq + qt[p], 0)),
            pl.BlockSpec((2, tq, DIFF_DK), lambda b, h, p, qt, kt: (h, b * nq + kt[p], 0)),
            pl.BlockSpec((tq, DIFF_DV), lambda b, h, p, qt, kt: (b * nq + kt[p], h)),
            pl.BlockSpec((4, DIFF_DK), lambda b, h, p, qt, kt: (0, 0)),
            pl.BlockSpec((1, DIFF_DV), lambda b, h, p, qt, kt: (0, 0)),
        ],
        out_specs=pl.BlockSpec((tq, 2 * DIFF_DV), lambda b, h, p, qt, kt: (b * nq + qt[p], h)),
        scratch_shapes=[pltpu.VMEM((2, rows, 1), F32), pltpu.VMEM((2, rows, 1), F32),
                        pltpu.VMEM((2, rows, DIFF_DV), F32)],
    )
    return pl.pallas_call(
        functools.partial(_flash_kernel, mode='diff', tq=tq, lam_init=lam_init), grid_spec=grid_spec,
        out_shape=jax.ShapeDtypeStruct((nb * t, 4 * DIFF_DV), F32),
        compiler_params=_cparams("parallel", "parallel", "arbitrary"), name="prompt_diff_attn",
    )(qi_tab, ki_tab, qd, kh, vbf, lw['diff_lambda'], lw['diff_subln'])


def _prompt_mla_attn(qm, kvbf, nb, t, lw):
    tq = _tile(t, 256)
    nq = t // tq
    qi_tab, ki_tab = _pair_tables(nq)
    rows = MLA_HEADS * tq
    grid_spec = pltpu.PrefetchScalarGridSpec(
        num_scalar_prefetch=2, grid=(nb, len(qi_tab)),
        in_specs=[
            pl.BlockSpec((MLA_HEADS, tq, MLA_CACHE_W), lambda b, p, qt, kt: (0, b * nq + qt[p], 0)),
            pl.BlockSpec((tq, MLA_CACHE_W), lambda b, p, qt, kt: (b * nq + kt[p], 0)),
            pl.BlockSpec((MLA_HEADS, MLA_KV_LORA, MLA_V), lambda b, p, qt, kt: (0, 0, 0)),
        ],
        out_specs=pl.BlockSpec((tq, MLA_HEADS * MLA_V), lambda b, p, qt, kt: (b * nq + qt[p], 0)),
        scratch_shapes=[pltpu.VMEM((1, rows, 1), F32), pltpu.VMEM((1, rows, 1), F32),
                        pltpu.VMEM((1, rows, MLA_KV_LORA), F32)],
    )
    return pl.pallas_call(
        functools.partial(_flash_kernel, mode='mla', tq=tq, lam_init=0.0), grid_spec=grid_spec,
        out_shape=jax.ShapeDtypeStruct((nb * t, MLA_HEADS * MLA_V), F32),
        compiler_params=_cparams("parallel", "arbitrary"), name="prompt_mla_attn",
    )(qi_tab, ki_tab, qm, kvbf, lw['wuv'])


def _decode_kernel(pt_ref, *refs, mode, npg, tn, lam_init):
    del pt_ref
    if mode == 'diff':
        q_ref = refs[0]
        k_refs = refs[1:1 + npg]
        v_refs = refs[1 + npg:1 + 2 * npg]
        kn_ref, vn_ref, lam_ref, g_ref, o_ref, m_sc, l_sc, acc_sc = refs[1 + 2 * npg:]
    else:
        q_ref = refs[0]
        k_refs = refs[1:1 + npg]
        kn_ref, wuv_ref, o_ref, m_sc, l_sc, acc_sc = refs[1 + npg:]
    c = pl.program_id(1)

    @pl.when(c == 0)
    def _():
        m_sc[...] = jnp.full(m_sc.shape, NEG_BIG, F32)
        l_sc[...] = jnp.zeros(l_sc.shape, F32)
        acc_sc[...] = jnp.zeros(acc_sc.shape, F32)

    q = q_ref[0]
    ks = [kr[0, 0].astype(BF16) for kr in k_refs]
    if mode == 'diff':
        vs = [vr[0, 0].astype(BF16) for vr in v_refs]
    else:
        vs = [kj[:, :MLA_KV_LORA] for kj in ks]
    s = jnp.concatenate([_dot_nt(q, kj) for kj in ks], axis=1)
    m_prev = m_sc[...]
    m_new = jnp.maximum(m_prev, jnp.max(s, axis=-1, keepdims=True))
    alpha = jnp.exp(m_prev - m_new)
    p = jnp.exp(s - m_new).astype(BF16)
    l_sc[...] = alpha * l_sc[...] + jnp.sum(p.astype(F32), axis=-1, keepdims=True)
    pv = _dot(p[:, :PAGE_SIZE], vs[0])
    for j in range(1, npg):
        pv = pv + _dot(p[:, j * PAGE_SIZE:(j + 1) * PAGE_SIZE], vs[j])
    acc_sc[...] = alpha * acc_sc[...] + pv
    m_sc[...] = m_new

    @pl.when(c == pl.num_programs(1) - 1)
    def _():
        qf = q.astype(F32)
        kn = kn_ref[0].astype(BF16).astype(F32)
        vn = (vn_ref[0] if mode == 'diff' else kn_ref[0][:, :MLA_KV_LORA]).astype(BF16).astype(F32)
        tok = lax.broadcasted_iota(jnp.int32, (q.shape[0], 1), 0) % tn
        sn = [jnp.where(tok >= j, jnp.sum(qf * kn[j:j + 1], axis=-1, keepdims=True), NEG_BIG) for j in range(tn)]
        m_prev = m_sc[...]
        m_new = m_prev
        for sj in sn:
            m_new = jnp.maximum(m_new, sj)
        alpha = jnp.exp(m_prev - m_new)
        l = alpha * l_sc[...]
        acc = alpha * acc_sc[...]
        for j, sj in enumerate(sn):
            pj = jnp.exp(sj - m_new)
            l = l + pj
            acc = acc + pj * vn[j:j + 1]
        o = acc / l
        if mode == 'diff':
            half = q.shape[0] // 2
            lam = _diff_lambda(lam_ref, lam_init)
            od = o[:half] - lam * o[half:]
            for h in range(2):
                oh = od[h * 2 * tn:(h + 1) * 2 * tn, h * DIFF_DV:(h + 1) * DIFF_DV]
                y = _rms(oh, g_ref[...], SUBLN_EPS) * (1.0 - lam_init)
                for r in range(2):
                    col = (h * 2 + r) * DIFF_DV
                    o_ref[0, :, col:col + DIFF_DV] = y[r * tn:(r + 1) * tn]
        else:
            ob = o.astype(BF16)
            for h in range(MLA_HEADS):
                o_ref[0, :, h * MLA_V:(h + 1) * MLA_V] = _dot(ob[h * tn:(h + 1) * tn], wuv_ref[h])


def _sample_attn(mode, layer, q, cache_k, cache_v, knew, vnew, page_table, lw, lam_init):
    nb, rows, dk = q.shape
    n_pages = page_table.shape[1]
    npg = min(PAGES_PER_STEP, n_pages)
    assert n_pages % npg == 0
    tn = knew.shape[1]
    page = lambda j: (lambda b, c, pt: (layer, pt[b, c * npg + j], 0, 0))
    seq3 = lambda b, c, pt: (b, 0, 0)
    in_specs = [pl.BlockSpec((1, rows, dk), seq3)]
    in_specs += [pl.BlockSpec((1, 1, PAGE_SIZE, cache_k.shape[-1]), page(j)) for j in range(npg)]
    operands = [q] + [cache_k] * npg
    if mode == 'diff':
        in_specs += [pl.BlockSpec((1, 1, PAGE_SIZE, cache_v.shape[-1]), page(j)) for j in range(npg)]
        operands += [cache_v] * npg
        in_specs += [pl.BlockSpec((1, tn, knew.shape[-1]), seq3), pl.BlockSpec((1, tn, vnew.shape[-1]), seq3),
                     pl.BlockSpec((4, DIFF_DK), lambda b, c, pt: (0, 0)),
                     pl.BlockSpec((1, DIFF_DV), lambda b, c, pt: (0, 0))]
        operands += [knew, vnew, lw['diff_lambda'], lw['diff_subln']]
        dv, width = DIFF_V_W, 4 * DIFF_DV
    else:
        in_specs += [pl.BlockSpec((1, tn, knew.shape[-1]), seq3),
                     pl.BlockSpec((MLA_HEADS, MLA_KV_LORA, MLA_V), lambda b, c, pt: (0, 0, 0))]
        operands += [knew, lw['wuv']]
        dv, width = MLA_KV_LORA, MLA_HEADS * MLA_V
    grid_spec = pltpu.PrefetchScalarGridSpec(
        num_scalar_prefetch=1, grid=(nb, n_pages // npg), in_specs=in_specs,
        out_specs=pl.BlockSpec((1, tn, width), seq3),
        scratch_shapes=[pltpu.VMEM((rows, 1), F32), pltpu.VMEM((rows, 1), F32), pltpu.VMEM((rows, dv), F32)],
    )
    return pl.pallas_call(
        functools.partial(_decode_kernel, mode=mode, npg=npg, tn=tn, lam_init=lam_init), grid_spec=grid_spec,
        out_shape=jax.ShapeDtypeStruct((nb, tn, width), F32),
        compiler_params=_cparams("parallel", "arbitrary"), name="sample_%s_attn" % mode,
    )(page_table, *operands)


def _mix_out_kernel(x_ref, yrw_ref, yd_ref, ym_ref, w1, w2, w3, gx_ref, wxq, h_o, qx_o):
    h = (x_ref[...] + _dot(yrw_ref[...].astype(BF16), w1[...]) + _dot(yd_ref[...].astype(BF16), w2[...])
         + _dot(ym_ref[...].astype(BF16), w3[...]))
    h_o[...] = h
    qx_o[...] = _dot(_rms(h, gx_ref[...], RMS_EPS).astype(BF16), wxq[...])


def _mix_out(x, yrw, yd, ym, lw):
    n = x.shape[0]
    tm = _tile(n, 512)
    row = lambda w: pl.BlockSpec((tm, w), lambda i: (i, 0))
    ws = [lw['w_out_rw'], lw['w_out_d'], lw['w_out_m'], lw['g_cross'], lw['w_xq']]
    return pl.pallas_call(
        _mix_out_kernel, grid=(n // tm,),
        in_specs=[row(D_MODEL), row(RWKV_W), row(4 * DIFF_DV), row(MLA_HEADS * MLA_V)] + [_full(w.shape) for w in ws],
        out_specs=[row(D_MODEL), row(MEM_W)],
        out_shape=[jax.ShapeDtypeStruct((n, D_MODEL), F32), jax.ShapeDtypeStruct((n, MEM_W), F32)],
        compiler_params=_cparams("parallel"), name="mix_out",
    )(x, yrw, yd, ym, *ws)


def _mem_kv_kernel(x_ref, g_ref, wk, wv, k_o, v_o):
    n = _rms(x_ref[...], g_ref[...], RMS_EPS).astype(BF16)
    k_o[...] = _dot(n, wk[...])
    v_o[...] = _dot(n, wv[...])


def _mem_kv(mem, lw):
    n = mem.shape[0]
    tm = _tile(n, 512)
    row = lambda w: pl.BlockSpec((tm, w), lambda i: (i, 0))
    return pl.pallas_call(
        _mem_kv_kernel, grid=(n // tm,),
        in_specs=[row(D_MODEL), _full((1, D_MODEL)), _full(lw['w_xk'].shape), _full(lw['w_xv'].shape)],
        out_specs=[row(MEM_W), row(MEM_W)],
        out_shape=[jax.ShapeDtypeStruct((n, MEM_W), F32)] * 2,
        compiler_params=_cparams("parallel"), name="mem_kv",
    )(mem, lw['g_mem'], lw['w_xk'], lw['w_xv'])


def _cross_kernel(q_ref, k_ref, v_ref, o_ref):
    q = q_ref[...].astype(BF16)
    k = k_ref[0, 0].astype(BF16)
    v = v_ref[0, 0].astype(BF16)
    for h in range(MEM_HEADS):
        sl = slice(h * MEM_HEAD_DIM, (h + 1) * MEM_HEAD_DIM)
        s = _dot_nt(q[:, sl], k[:, sl])
        p = jnp.exp(s - jnp.max(s, axis=-1, keepdims=True))
        p = p / jnp.sum(p, axis=-1, keepdims=True)
        o_ref[:, sl] = _dot(p.astype(BF16), v[:, sl])


def _cross_attn(qx, row0, nseq, t, mk, mv, layer):
    tq = _tile(t, 512)
    nt = t // tq
    blk0 = row0 // tq
    mem = pl.BlockSpec((1, 1) + mk.shape[2:], lambda s, i: (layer, s, 0, 0))
    return pl.pallas_call(
        _cross_kernel, grid=(nseq, nt),
        in_specs=[pl.BlockSpec((tq, MEM_W), lambda s, i: (blk0 + s * nt + i, 0)), mem, mem],
        out_specs=pl.BlockSpec((tq, MEM_W), lambda s, i: (s * nt + i, 0)),
        out_shape=jax.ShapeDtypeStruct((nseq * t, MEM_W), F32),
        compiler_params=_cparams("parallel", "arbitrary"), name="cross_attn",
    )(qx, mk, mv)


def _ffn_kernel(h_ref, ox_ref, wxo, g_ref, wg, wu, wd, o_ref, h2_sc, n_sc, acc_sc):
    f = pl.program_id(1)

    @pl.when(f == 0)
    def _():
        h2 = h_ref[...] + _dot(ox_ref[...].astype(BF16), wxo[...])
        h2_sc[...] = h2
        n_sc[...] = _rms(h2, g_ref[...], RMS_EPS).astype(BF16)
        acc_sc[...] = jnp.zeros(acc_sc.shape, F32)

    n = n_sc[...]
    hg = _dot(n, wg[...])
    acc_sc[...] += _dot((jax.nn.silu(hg) * _dot(n, wu[...])).astype(BF16), wd[...])

    @pl.when(f == pl.num_programs(1) - 1)
    def _():
        o_ref[...] = h2_sc[...] + acc_sc[...]


def _ffn_dense(h, ox, lw):
    n = h.shape[0]
    tm = _tile(n, 1024)
    dff = lw['ffn_wg'].shape[1]
    tf = 256
    assert dff % tf == 0
    row = lambda w: pl.BlockSpec((tm, w), lambda i, f: (i, 0))
    return pl.pallas_call(
        _ffn_kernel, grid=(n // tm, dff // tf),
        in_specs=[row(D_MODEL), row(MEM_W), pl.BlockSpec((MEM_W, D_MODEL), lambda i, f: (0, 0)),
                  pl.BlockSpec((1, D_MODEL), lambda i, f: (0, 0)),
                  pl.BlockSpec((D_MODEL, tf), lambda i, f: (0, f)), pl.BlockSpec((D_MODEL, tf), lambda i, f: (0, f)),
                  pl.BlockSpec((tf, D_MODEL), lambda i, f: (f, 0))],
        out_specs=row(D_MODEL), out_shape=jax.ShapeDtypeStruct((n, D_MODEL), F32),
        scratch_shapes=[pltpu.VMEM((tm, D_MODEL), F32), pltpu.VMEM((tm, D_MODEL), BF16),
                        pltpu.VMEM((tm, D_MODEL), F32)],
        compiler_params=_cparams("parallel", "arbitrary"), name="ffn_dense",
    )(h, ox, lw['w_xo'], lw['g_ffn'], lw['ffn_wg'], lw['ffn_wu'], lw['ffn_wd'])


def _router_kernel(h_ref, ox_ref, wxo, g_ref, wr, h2_o, n_o, comb_o, pos_o, cnt_o):
    h2 = h_ref[...] + _dot(ox_ref[...].astype(BF16), wxo[...])
    h2_o[...] = h2
    nf = _rms(h2, g_ref[...], RMS_EPS)
    n_o[...] = nf.astype(BF16)
    logits = _dot_hi(nf, wr[...])
    tm = logits.shape[0]
    eid = lax.broadcasted_iota(jnp.int32, logits.shape, 1)
    m1 = jnp.max(logits, axis=-1, keepdims=True)
    i1 = jnp.min(jnp.where(logits == m1, eid, N_EXPERTS), axis=-1, keepdims=True)
    sel1 = eid == i1
    rest = jnp.where(sel1, -jnp.inf, logits)
    m2 = jnp.max(rest, axis=-1, keepdims=True)
    i2 = jnp.min(jnp.where(rest == m2, eid, N_EXPERTS), axis=-1, keepdims=True)
    sel2 = eid == i2
    e2 = jnp.exp(m2 - m1)
    den = 1.0 + e2
    comb_o[...] = jnp.where(sel1, 1.0 / den, 0.0) + jnp.where(sel2, e2 / den, 0.0)
    sel = jnp.logical_or(sel1, sel2).astype(F32)
    lower = (lax.broadcasted_iota(jnp.int32, (tm, tm), 1) < lax.broadcasted_iota(jnp.int32, (tm, tm), 0))
    rank = _dot(lower.astype(BF16), sel.astype(BF16))
    pos_o[...] = jnp.where(sel > 0.0, rank, -1.0)
    cnt_o[0] = jnp.sum(sel, axis=0, keepdims=True)


def _router(h, ox, lw, tm):
    n = h.shape[0]
    row = lambda w: pl.BlockSpec((tm, w), lambda i: (i, 0))
    return pl.pallas_call(
        _router_kernel, grid=(n // tm,),
        in_specs=[row(D_MODEL), row(MEM_W), _full((MEM_W, D_MODEL)), _full((1, D_MODEL)),
                  _full((D_MODEL, N_EXPERTS))],
        out_specs=[row(D_MODEL), row(D_MODEL), row(N_EXPERTS), row(N_EXPERTS),
                   pl.BlockSpec((1, 1, N_EXPERTS), lambda i: (i, 0, 0))],
        out_shape=[jax.ShapeDtypeStruct((n, D_MODEL), F32), jax.ShapeDtypeStruct((n, D_MODEL), BF16),
                   jax.ShapeDtypeStruct((n, N_EXPERTS), F32), jax.ShapeDtypeStruct((n, N_EXPERTS), F32),
                   jax.ShapeDtypeStruct((n // tm, 1, N_EXPERTS), F32)],
        compiler_params=_cparams("parallel"), name="moe_router",
    )(h, ox, lw['w_xo'], lw['g_ffn'], lw['moe_router'])


def _moe_kernel(cnt_ref, n_ref, post_ref, pos_ref, comb_ref, h2_ref, wg, wu, wd, o_ref, xe_sc, acc_sc):
    i = pl.program_id(0)
    e = pl.program_id(1)
    f = pl.program_id(2)
    tm = n_ref.shape[0]
    ch = MOE_CHUNK
    nch = (cnt_ref[i * N_EXPERTS + e] + ch - 1) // ch

    @pl.when(jnp.logical_and(e == 0, f == 0))
    def _():
        o_ref[...] = h2_ref[...]

    @pl.when(f == 0)
    def _():
        esub = lax.broadcasted_iota(jnp.int32, (N_EXPERTS, tm), 0)
        pos_row = jnp.sum(jnp.where(esub == e, post_ref[...], 0.0), axis=0, keepdims=True)
        slot = lax.broadcasted_iota(jnp.int32, (ch, tm), 0).astype(F32)

        def gather(c, carry):
            base = pl.multiple_of(c * ch, ch)
            onehot = (pos_row == slot + (c * ch).astype(F32)).astype(BF16)
            xe_sc[pl.ds(base, ch), :] = _dot(onehot, n_ref[...]).astype(BF16)
            acc_sc[pl.ds(base, ch), :] = jnp.zeros((ch, D_MODEL), F32)
            return carry

        lax.fori_loop(0, nch, gather, 0)

    def expert(c, carry):
        base = pl.multiple_of(c * ch, ch)
        x = xe_sc[pl.ds(base, ch), :]
        hh = (jax.nn.silu(_dot(x, wg[0])) * _dot(x, wu[0])).astype(BF16)
        acc_sc[pl.ds(base, ch), :] += _dot(hh, wd[0])
        return carry

    lax.fori_loop(0, nch, expert, 0)

    @pl.when(f == pl.num_programs(2) - 1)
    def _():
        elane = lax.broadcasted_iota(jnp.int32, (tm, N_EXPERTS), 1)
        pos_col = jnp.sum(jnp.where(elane == e, pos_ref[...], 0.0), axis=-1, keepdims=True)
        gate_col = jnp.sum(jnp.where(elane == e, comb_ref[...], 0.0), axis=-1, keepdims=True)
        slot = lax.broadcasted_iota(jnp.int32, (tm, ch), 1).astype(F32)

        def scatter(c, carry):
            base = pl.multiple_of(c * ch, ch)
            onehot = (pos_col == slot + (c * ch).astype(F32)).astype(BF16)
            y = acc_sc[pl.ds(base, ch), :].astype(BF16)
            o_ref[...] += gate_col * _dot(onehot, y)
            return carry

        lax.fori_loop(0, nch, scatter, 0)


def _moe(h, ox, lw):
    n = h.shape[0]
    tm = _tile(n, 1024)
    h2, nb, comb, pos, cnt = _router(h, ox, lw, tm)
    pos_t = pos.T
    cnt = cnt.reshape(-1).astype(jnp.int32)
    dff = lw['moe_wg'].shape[2]
    tf = 512
    assert dff % tf == 0
    tok = lambda w: pl.BlockSpec((tm, w), lambda i, e, f, c: (i, 0))
    grid_spec = pltpu.PrefetchScalarGridSpec(
        num_scalar_prefetch=1, grid=(n // tm, N_EXPERTS, dff // tf),
        in_specs=[tok(D_MODEL), pl.BlockSpec((N_EXPERTS, tm), lambda i, e, f, c: (0, i)),
                  tok(N_EXPERTS), tok(N_EXPERTS), tok(D_MODEL),
                  pl.BlockSpec((1, D_MODEL, tf), lambda i, e, f, c: (e, 0, f)),
                  pl.BlockSpec((1, D_MODEL, tf), lambda i, e, f, c: (e, 0, f)),
                  pl.BlockSpec((1, tf, D_MODEL), lambda i, e, f, c: (e, f, 0))],
        out_specs=tok(D_MODEL),
        scratch_shapes=[pltpu.VMEM((tm, D_MODEL), BF16), pltpu.VMEM((tm, D_MODEL), F32)],
    )
    return pl.pallas_call(
        _moe_kernel, grid_spec=grid_spec, out_shape=jax.ShapeDtypeStruct((n, D_MODEL), F32),
        compiler_params=_cparams("parallel", "arbitrary", "arbitrary"), name="moe_experts",
    )(cnt, nb, pos_t, pos, comb, h2, lw['moe_wg'], lw['moe_wu'], lw['moe_wd'])


def _final_norm_kernel(x_ref, g_ref, o_ref):
    o_ref[...] = _rms(x_ref[...], g_ref[...], RMS_EPS)


def _final_norm(x, row0, rows, g):
    tm = _tile(rows, 1024)
    blk0 = row0 // tm
    return pl.pallas_call(
        _final_norm_kernel, grid=(rows // tm,),
        in_specs=[pl.BlockSpec((tm, D_MODEL), lambda i: (blk0 + i, 0)), _full((1, D_MODEL))],
        out_specs=pl.BlockSpec((tm, D_MODEL), lambda i: (i, 0)),
        out_shape=jax.ShapeDtypeStruct((rows, D_MODEL), F32),
        compiler_params=_cparams("parallel"), name="final_norm",
    )(x, g)


def _rot_cols(w):
    half = w.shape[-1] // 2
    return jnp.concatenate([-w[..., half:], w[..., :half]], axis=-1)


def _layer_weights(p, l):
    bf = lambda a: a.astype(BF16)
    r2 = lambda a: a.reshape(1, -1).astype(F32)
    w_in = p['w_in'][l]
    c1 = RWKV_IN
    c2 = c1 + DIFF_Q_W
    c3 = c2 + DIFF_K_W
    c4 = c3 + DIFF_V_W
    c5 = c4 + MLA_Q_LORA
    c6 = c5 + MLA_KV_LORA
    wq = w_in[:, c1:c2].reshape(D_MODEL, 2, 2, 2, DIFF_DK).transpose(0, 1, 3, 2, 4).reshape(D_MODEL, DIFF_Q_W)
    wuq = p['mla_wuq'][l]
    w_out = p['w_out'][l]
    lw = dict(
        g_mix=r2(p['norm_mix'][l]), w_rw=bf(w_in[:, :c1]), w_q=bf(wq * DIFF_DK ** -0.5), w_k=bf(w_in[:, c2:c3]),
        w_v=bf(w_in[:, c3:c4]), w_cq=bf(w_in[:, c4:c5]), w_ckv=bf(w_in[:, c5:c6]), w_kr=bf(w_in[:, c6:]),
        w_krr=bf(_rot_cols(w_in[:, c6:])), g_q=r2(p['mla_qnorm'][l]), g_kv=r2(p['mla_kvnorm'][l]),
        wq_nope=bf(wuq[:, :, :MLA_NOPE].transpose(1, 0, 2)), wq_rope=bf(wuq[:, :, MLA_NOPE:].transpose(1, 0, 2)),
        wq_rope_rot=bf(_rot_cols(wuq[:, :, MLA_NOPE:]).transpose(1, 0, 2)),
        wuk_t=bf(p['mla_wuk'][l].transpose(1, 2, 0)), wuv=bf(p['mla_wuv'][l].transpose(1, 0, 2)),
        rw_mu=r2(p['rwkv_mu'][l]), rw_w0=r2(p['rwkv_w0'][l]), rw_a0=r2(p['rwkv_a0'][l]), rw_kk=r2(p['rwkv_kk'][l]),
        rw_ka=r2(p['rwkv_ka'][l]), rw_rk=r2(p['rwkv_rk'][l]), rw_w2=bf(p['rwkv_w2'][l]), rw_a2=bf(p['rwkv_a2'][l]),
        rw_g2=bf(p['rwkv_g2'][l]), rw_lnw=r2(p['rwkv_lnw'][l]), rw_lnb=r2(p['rwkv_lnb'][l]),
        bones=jnp.kron(jnp.eye(RWKV_HEADS, dtype=F32), jnp.ones((RWKV_N, RWKV_N), F32)),
        diff_lambda=p['diff_lambda'][l].astype(F32), diff_subln=r2(p['diff_subln'][l]),
        w_out_rw=bf(w_out[:RWKV_W]), w_out_d=bf(w_out[RWKV_W:RWKV_W + 4 * DIFF_DV]),
        w_out_m=bf(w_out[RWKV_W + 4 * DIFF_DV:]), g_cross=r2(p['norm_cross'][l]),
        w_xq=bf(p['w_xq'][l] * MEM_HEAD_DIM ** -0.5), g_mem=r2(p['norm_mem'][l]), w_xk=bf(p['w_xk'][l]),
        w_xv=bf(p['w_xv'][l]), w_xo=bf(p['w_xo'][l]), g_ffn=r2(p['norm_ffn'][l]),
    )
    i = l // 2
    if l % 2 == 0:
        lw.update(ffn_wg=bf(p['ffn_wg'][i]), ffn_wu=bf(p['ffn_wu'][i]), ffn_wd=bf(p['ffn_wd'][i]))
    else:
        lw.update(moe_router=p['moe_router'][i].astype(F32), moe_wg=bf(p['moe_wg'][i]), moe_wu=bf(p['moe_wu'][i]),
                  moe_wd=bf(p['moe_wd'][i]))
    return lw


def _rope_tables(pos):
    half = MLA_ROPE // 2
    inv = ROPE_THETA ** (-jnp.arange(half, dtype=F32) / half)
    ang = pos.astype(F32)[:, None] * inv[None, :]
    cos, sin = jnp.cos(ang), jnp.sin(ang)
    return jnp.concatenate([cos, cos], axis=-1), jnp.concatenate([sin, sin], axis=-1)


def _sample_diff_queries(qd, row0, nb, tn):
    q = qd[:, row0:].reshape(2, 2, 2, nb, tn, DIFF_DK)
    q = q.transpose(3, 1, 0, 2, 4, 5)
    two = jnp.arange(2)
    place = (two[None, :, None, None] == two[None, None, :, None]) & \
            (two[:, None, None, None] == two[None, None, None, :])
    out = q[:, :, :, :, :, None, None, :] * place.astype(q.dtype)[None, :, :, None, None, :, :, None]
    return out.reshape(nb, 8 * tn, 4 * DIFF_DK)


def kernel(x_prompt, x_sample, state_rwkv, state_shift, cache_diff_k, cache_diff_v, cache_mla_kv, cache_mem_k, cache_mem_v, page_table, mem_prompt, norm_mix, w_in, rwkv_mu, rwkv_w0, rwkv_w2, rwkv_a0, rwkv_a2, rwkv_g2, rwkv_kk, rwkv_ka, rwkv_rk, rwkv_lnw, rwkv_lnb, diff_lambda, diff_subln, mla_qnorm, mla_kvnorm, mla_wuq, mla_wuk, mla_wuv, w_out, norm_cross, norm_mem, w_xq, w_xk, w_xv, w_xo, norm_ffn, ffn_wg, ffn_wu, ffn_wd, moe_router, moe_wg, moe_wu, moe_wd, final_norm):
    p = dict(norm_mix=norm_mix, w_in=w_in, rwkv_mu=rwkv_mu, rwkv_w0=rwkv_w0, rwkv_w2=rwkv_w2, rwkv_a0=rwkv_a0,
             rwkv_a2=rwkv_a2, rwkv_g2=rwkv_g2, rwkv_kk=rwkv_kk, rwkv_ka=rwkv_ka, rwkv_rk=rwkv_rk,
             rwkv_lnw=rwkv_lnw, rwkv_lnb=rwkv_lnb, diff_lambda=diff_lambda, diff_subln=diff_subln,
             mla_qnorm=mla_qnorm, mla_kvnorm=mla_kvnorm, mla_wuq=mla_wuq, mla_wuk=mla_wuk, mla_wuv=mla_wuv,
             w_out=w_out, norm_cross=norm_cross, norm_mem=norm_mem, w_xq=w_xq, w_xk=w_xk, w_xv=w_xv, w_xo=w_xo,
             norm_ffn=norm_ffn, ffn_wg=ffn_wg, ffn_wu=ffn_wu, ffn_wd=ffn_wd, moe_router=moe_router, moe_wg=moe_wg,
             moe_wu=moe_wu, moe_wd=moe_wd)
    bp, t = x_prompt.shape[:2]
    bs, tn = x_sample.shape[:2]
    depth = w_in.shape[0]
    n_p, n_s = bp * t, bs * tn
    past = page_table.shape[1] * PAGE_SIZE
    mem_len = mem_prompt.shape[1]

    x = jnp.concatenate([x_prompt.reshape(n_p, D_MODEL), x_sample.reshape(n_s, D_MODEL)], axis=0)
    pos = jnp.concatenate([jnp.tile(jnp.arange(t), bp), jnp.tile(past + jnp.arange(tn), bs)])
    cos, sin = _rope_tables(pos)
    n_pool = cache_diff_k.shape[1]
    ck = cache_diff_k.reshape(depth, n_pool, PAGE_SIZE, DIFF_K_W)
    cv = cache_diff_v.reshape(depth, n_pool, PAGE_SIZE, DIFF_V_W)
    cmk = cache_mem_k.reshape(depth, bs, mem_len, MEM_W)
    cmv = cache_mem_v.reshape(depth, bs, mem_len, MEM_W)
    mem = mem_prompt.reshape(bp * mem_len, D_MODEL)
    zeros_shift = jnp.zeros((bp, RWKV_IN), F32)
    zeros_state = jnp.zeros((bp, RWKV_HEADS, RWKV_N, RWKV_N), F32)

    outs = {k: [] for k in ('pS', 'psh', 'pdk', 'pdv', 'pkv', 'pmk', 'pmv', 'sS', 'ssh', 'sdk', 'sdv', 'skv')}
    for l in range(depth):
        lw = _layer_weights(p, l)
        lam_init = 0.8 - 0.6 * math.exp(-0.3 * l)
        xrw, qd, ktok, kh, vtok, vbf, kv, kvbf, qm = _mixer_in(x, lw, cos, sin)

        yrw_p, st_p = _rwkv_scan(_rwkv_prep(xrw, 0, bp, t, zeros_shift, lw), zeros_state, bp, t, lw)
        yrw_s, st_s = _rwkv_scan(_rwkv_prep(xrw, n_p, bs, tn, state_shift[l], lw), state_rwkv[l], bs, tn, lw)

        yd_p = _prompt_diff_attn(qd, kh, vbf, bp, t, lw, lam_init)
        yd_s = _sample_attn('diff', l, _sample_diff_queries(qd, n_p, bs, tn), ck, cv,
                            ktok[n_p:].reshape(bs, tn, DIFF_K_W), vtok[n_p:].reshape(bs, tn, DIFF_V_W),
                            page_table, lw, lam_init)

        ym_p = _prompt_mla_attn(qm, kvbf, bp, t, lw)
        qm_s = qm[:, n_p:].reshape(MLA_HEADS, bs, tn, MLA_CACHE_W).transpose(1, 0, 2, 3)
        ym_s = _sample_attn('mla', l, qm_s.reshape(bs, MLA_HEADS * tn, MLA_CACHE_W), cache_mla_kv, None,
                            kv[n_p:].reshape(bs, tn, MLA_CACHE_W), None, page_table, lw, lam_init)

        cat = lambda a, b: jnp.concatenate([a, b.reshape(n_s, -1)], axis=0)
        h, qx = _mix_out(x, cat(yrw_p, yrw_s), cat(yd_p, yd_s), cat(ym_p, ym_s), lw)

        mk, mv = _mem_kv(mem, lw)
        ox_p = _cross_attn(qx, 0, bp, t, mk.reshape(1, bp, mem_len, MEM_W), mv.reshape(1, bp, mem_len, MEM_W), 0)
        ox_s = _cross_attn(qx, n_p, bs, tn, cmk, cmv, l)
        ox = jnp.concatenate([ox_p, ox_s], axis=0)

        x = _ffn_dense(h, ox, lw) if l % 2 == 0 else _moe(h, ox, lw)

        outs['pS'].append(st_p)
        outs['psh'].append(xrw[:n_p].reshape(bp, t, RWKV_IN)[:, -1])
        outs['pdk'].append(ktok[:n_p].reshape(bp, t, 2, 2, DIFF_DK))
        outs['pdv'].append(vtok[:n_p].reshape(bp, t, 2, DIFF_DV))
        outs['pkv'].append(kv[:n_p].reshape(bp, t, MLA_CACHE_W))
        outs['pmk'].append(mk.reshape(bp, mem_len, MEM_HEADS, MEM_HEAD_DIM))
        outs['pmv'].append(mv.reshape(bp, mem_len, MEM_HEADS, MEM_HEAD_DIM))
        outs['sS'].append(st_s)
        outs['ssh'].append(xrw[n_p:].reshape(bs, tn, RWKV_IN)[:, -1])
        outs['sdk'].append(ktok[n_p:].reshape(bs, tn, 2, 2, DIFF_DK))
        outs['sdv'].append(vtok[n_p:].reshape(bs, tn, 2, DIFF_DV))
        outs['skv'].append(kv[n_p:].reshape(bs, tn, MLA_CACHE_W))

    g = final_norm.reshape(1, D_MODEL).astype(F32)
    y_prompt = _final_norm(x, 0, n_p, g).reshape(bp, t, D_MODEL)
    y_sample = _final_norm(x, n_p, n_s, g).reshape(bs, tn, D_MODEL)
    st = lambda k: jnp.stack(outs[k])
    return (y_prompt, y_sample, st('pS'), st('psh'), st('pdk'), st('pdv'), st('pkv'), st('pmk'), st('pmv'),
            st('sS'), st('ssh'), st('sdk'), st('sdv'), st('skv'))
```

```python
import functools
import math

import jax
import jax.numpy as jnp
from jax import lax
from jax.experimental import pallas as pl
from jax.experimental.pallas import tpu as pltpu

F32 = jnp.float32
BF16 = jnp.bfloat16

D_MODEL = 1024
PAGE_SIZE = 128
RWKV_HEADS = 4
RWKV_N = 64
RWKV_W = 256
DECAY_LORA = 64
ICLR_LORA = 64
GATE_LORA = 128
RWKV_IN = 1024
GN_EPS = 64e-5
DIFF_DK = 64
DIFF_DV = 128
DIFF_Q_W = 512
DIFF_K_W = 256
DIFF_V_W = 256
SUBLN_EPS = 1e-5
MLA_HEADS = 4
MLA_NOPE = 64
MLA_ROPE = 32
MLA_V = 64
MLA_Q_LORA = 192
MLA_KV_LORA = 128
MLA_CACHE_W = 160
ROPE_THETA = 10000.0
MEM_HEADS = 4
MEM_HEAD_DIM = 64
MEM_W = 256
N_EXPERTS = 8
RMS_EPS = 1e-6
NEG_BIG = -1e30
LANES = 128

VMEM_LIMIT_BYTES = 56 * 1024 * 1024
MOE_CHUNK = 128
PAGES_PER_STEP = 16
RWKV_CHUNK = 64


def _cparams(*sem):
    return pltpu.CompilerParams(dimension_semantics=sem, vmem_limit_bytes=VMEM_LIMIT_BYTES)


def _tile(n, pref):
    t = pref
    while t > 8 and n % t:
        t //= 2
    assert n % t == 0, (n, pref)
    return t


def _dot(a, b):
    return jnp.dot(a, b, preferred_element_type=F32)


def _dot_nt(a, b):
    return lax.dot_general(a, b, (((1,), (1,)), ((), ())), preferred_element_type=F32)


def _dot_tn(a, b):
    return lax.dot_general(a, b, (((0,), (0,)), ((), ())), preferred_element_type=F32)


def _dot_hi(a, b):
    return jnp.dot(a, b, preferred_element_type=F32, precision=lax.Precision.HIGHEST)


def _rms(x, g, eps):
    return x * lax.rsqrt(jnp.mean(x * x, axis=-1, keepdims=True) + eps) * g


def _rep(x, n):
    return x if n == 1 else jnp.concatenate([x] * n, axis=1)


def _full(shape):
    nd = len(shape)
    return pl.BlockSpec(shape, lambda *a, _nd=nd: (0,) * _nd)


def _mixer_in_kernel(x_ref, g_ref, wrw, wq, wk, wv, wcq, wckv, wkr, wkrr, cos_ref, sin_ref, gq_ref, gkv_ref,
                     wqn, wqr, wqrr, wukt,
                     xrw_o, qd_o, ktok_o, kh_o, vtok_o, vbf_o, kv_o, kvbf_o, qm_o, *, mla_scale):
    n = _rms(x_ref[...], g_ref[...], RMS_EPS).astype(BF16)
    xrw_o[...] = _dot(n, wrw[...])
    q = _dot(n, wq[...]).astype(BF16)
    for j in range(8):
        qd_o[j] = q[:, j * DIFF_DK:(j + 1) * DIFF_DK]
    k = _dot(n, wk[...])
    ktok_o[...] = k
    kb = k.astype(BF16)
    for j in range(4):
        kh_o[j] = kb[:, j * DIFF_DK:(j + 1) * DIFF_DK]
    v = _dot(n, wv[...])
    vtok_o[...] = v
    vbf_o[...] = v.astype(BF16)
    cos = cos_ref[...]
    sin = sin_ref[...]
    ckvn = _rms(_dot(n, wckv[...]), gkv_ref[...], RMS_EPS)
    kr = _dot(n, wkr[...]) * cos + _dot(n, wkrr[...]) * sin
    kv_o[:, :MLA_KV_LORA] = ckvn
    kv_o[:, MLA_KV_LORA:] = kr
    kvbf_o[:, :MLA_KV_LORA] = ckvn.astype(BF16)
    kvbf_o[:, MLA_KV_LORA:] = kr.astype(BF16)
    cqn = _rms(_dot(n, wcq[...]), gq_ref[...], RMS_EPS).astype(BF16)
    for h in range(MLA_HEADS):
        qn = _dot(cqn, wqn[h]).astype(BF16)
        qlat = _dot(qn, wukt[h])
        qr = _dot(cqn, wqr[h]) * cos + _dot(cqn, wqrr[h]) * sin
        qm_o[h, :, :MLA_KV_LORA] = (qlat * mla_scale).astype(BF16)
        qm_o[h, :, MLA_KV_LORA:] = (qr * mla_scale).astype(BF16)


def _mixer_in(x, lw, cos, sin):
    n = x.shape[0]
    tm = _tile(n, 512)
    row = lambda w: pl.BlockSpec((tm, w), lambda i: (i, 0))
    slab = lambda s, w: pl.BlockSpec((s, tm, w), lambda i: (0, i, 0))
    ws = [lw['w_rw'], lw['w_q'], lw['w_k'], lw['w_v'], lw['w_cq'], lw['w_ckv'], lw['w_kr'], lw['w_krr']]
    tail = [lw['g_q'], lw['g_kv'], lw['wq_nope'], lw['wq_rope'], lw['wq_rope_rot'], lw['wuk_t']]
    in_specs = ([row(D_MODEL), _full((1, D_MODEL))] + [_full(w.shape) for w in ws]
                + [row(MLA_ROPE), row(MLA_ROPE)] + [_full(w.shape) for w in tail])
    out_shape = [
        jax.ShapeDtypeStruct((n, RWKV_IN), F32),
        jax.ShapeDtypeStruct((8, n, DIFF_DK), BF16),
        jax.ShapeDtypeStruct((n, DIFF_K_W), F32),
        jax.ShapeDtypeStruct((4, n, DIFF_DK), BF16),
        jax.ShapeDtypeStruct((n, DIFF_V_W), F32),
        jax.ShapeDtypeStruct((n, DIFF_V_W), BF16),
        jax.ShapeDtypeStruct((n, MLA_CACHE_W), F32),
        jax.ShapeDtypeStruct((n, MLA_CACHE_W), BF16),
        jax.ShapeDtypeStruct((MLA_HEADS, n, MLA_CACHE_W), BF16),
    ]
    out_specs = [row(RWKV_IN), slab(8, DIFF_DK), row(DIFF_K_W), slab(4, DIFF_DK), row(DIFF_V_W), row(DIFF_V_W),
                 row(MLA_CACHE_W), row(MLA_CACHE_W), slab(MLA_HEADS, MLA_CACHE_W)]
    return pl.pallas_call(
        functools.partial(_mixer_in_kernel, mla_scale=(MLA_NOPE + MLA_ROPE) ** -0.5),
        grid=(n // tm,), in_specs=in_specs, out_specs=out_specs, out_shape=out_shape,
        compiler_params=_cparams("parallel"), name="mixer_in",
    )(x, lw['g_mix'], *ws, cos, sin, *tail)


def _rwkv_prep_kernel(x_ref, prev_ref, s0_ref, mu_ref, w0_ref, a0_ref, kkw_ref, ka_ref, rk_ref, w2_ref, a2_ref,
                      g2_ref, bones_ref, r_o, lw_o, kf_o, v_o, nkk_o, kka_o, g_o, bonus_o):
    i = pl.program_id(1)
    x = x_ref[...]
    first = jnp.where(i == 0, s0_ref[0], prev_ref[7:8, :])
    rowid = lax.broadcasted_iota(jnp.int32, x.shape, 0)
    shifted = jnp.where(rowid == 0, first, pltpu.roll(x, 1, 0))
    xm = x + (shifted - x) * mu_ref[...]
    r = xm[:, 0:RWKV_W]
    k = xm[:, RWKV_W:2 * RWKV_W]
    v = xm[:, 2 * RWKV_W:3 * RWKV_W]
    c = 3 * RWKV_W
    xw = xm[:, c:c + DECAY_LORA]
    xa = xm[:, c + DECAY_LORA:c + DECAY_LORA + ICLR_LORA]
    xg = xm[:, c + DECAY_LORA + ICLR_LORA:]
    z = -(w0_ref[...] + _dot(jnp.tanh(xw).astype(BF16), w2_ref[...]))
    softplus = jnp.maximum(z, 0.0) + jnp.log(1.0 + jnp.exp(-jnp.abs(z)))
    logw = -jnp.exp(-softplus - 0.5)
    a = jax.nn.sigmoid(a0_ref[...] + _dot(xa.astype(BF16), a2_ref[...]))
    g = _dot(jax.nn.sigmoid(xg).astype(BF16), g2_ref[...])
    kkr = k * kkw_ref[...]
    kk = kkr / jnp.maximum(jnp.sqrt(_dot_hi(kkr * kkr, bones_ref[...])), 1e-12)
    kf = k * (1.0 + (a - 1.0) * ka_ref[...])
    bonus = _dot_hi(r * kf * rk_ref[...], bones_ref[...]) * v
    g_o[...] = g
    bonus_o[...] = bonus
    nkk = -kk
    kka = kk * a
    for h in range(RWKV_HEADS):
        sl = slice(h * RWKV_N, (h + 1) * RWKV_N)
        r_o[0, h] = r[:, sl]
        lw_o[0, h] = logw[:, sl]
        kf_o[0, h] = kf[:, sl]
        v_o[0, h] = v[:, sl]
        nkk_o[0, h] = nkk[:, sl]
        kka_o[0, h] = kka[:, sl]


def _rwkv_prep(xrw, row0, nb, t, shift0, lw):
    tm = _tile(t, 512)
    nt = t // tm
    blk0 = row0 // tm
    blk0_8 = row0 // 8
    in_specs = [
        pl.BlockSpec((tm, RWKV_IN), lambda b, i: (blk0 + b * nt + i, 0)),
        pl.BlockSpec((8, RWKV_IN), lambda b, i: (jnp.maximum(blk0_8 + (b * t + i * tm) // 8 - 1, 0), 0)),
        pl.BlockSpec((1, 1, RWKV_IN), lambda b, i: (b, 0, 0)),
    ]
    params = [lw['rw_mu'], lw['rw_w0'], lw['rw_a0'], lw['rw_kk'], lw['rw_ka'], lw['rw_rk'], lw['rw_w2'],
              lw['rw_a2'], lw['rw_g2'], lw['bones']]
    in_specs += [_full(p.shape) for p in params]
    hm = pl.BlockSpec((1, RWKV_HEADS, tm, RWKV_N), lambda b, i: (b, 0, i, 0))
    tokm = pl.BlockSpec((tm, RWKV_W), lambda b, i: (b * nt + i, 0))
    hm_shape = jax.ShapeDtypeStruct((nb, RWKV_HEADS, t, RWKV_N), F32)
    tok_shape = jax.ShapeDtypeStruct((nb * t, RWKV_W), F32)
    return pl.pallas_call(
        _rwkv_prep_kernel, grid=(nb, nt), in_specs=in_specs,
        out_specs=[hm] * 6 + [tokm] * 2, out_shape=[hm_shape] * 6 + [tok_shape] * 2,
        compiler_params=_cparams("parallel", "arbitrary"), name="rwkv_prep",
    )(xrw, xrw, shift0.reshape(nb, 1, RWKV_IN), *params)


def _rwkv_epilogue(y, h, g_ref, bonus_ref, lnw_ref, lnb_ref, y_o):
    sl = slice(h * RWKV_N, (h + 1) * RWKV_N)
    mu = jnp.mean(y, axis=-1, keepdims=True)
    yc = y - mu
    var = jnp.mean(yc * yc, axis=-1, keepdims=True)
    yn = yc * lax.rsqrt(var + GN_EPS) * lnw_ref[:, sl] + lnb_ref[:, sl]
    y_o[:, sl] = (yn + bonus_ref[:, sl]) * g_ref[:, sl]


def _rwkv_scan_kernel(r_ref, lw_ref, kf_ref, v_ref, nkk_ref, kka_ref, s0_ref, g_ref, bonus_ref, lnw_ref, lnb_ref,
                      y_o, st_o, s_sc, y_sc, *, tb):
    i = pl.program_id(1)

    @pl.when(i == 0)
    def _():
        s_sc[...] = s0_ref[0]

    eye = (lax.broadcasted_iota(jnp.int32, (RWKV_N, RWKV_N), 0)
           == lax.broadcasted_iota(jnp.int32, (RWKV_N, RWKV_N), 1)).astype(F32)

    def step(t, states):
        out = []
        for h in range(RWKV_HEADS):
            s = states[h]
            row = lambda ref: ref[0, h, pl.ds(t, 1), :]
            sa = jnp.sum(s * row(nkk_ref), axis=-1, keepdims=True)
            v_col = jnp.sum(eye * row(v_ref), axis=-1, keepdims=True)
            s = s * jnp.exp(row(lw_ref)) + sa * row(kka_ref) + v_col * row(kf_ref)
            y_col = jnp.sum(s * row(r_ref), axis=-1, keepdims=True)
            y_sc[h, pl.ds(t, 1), :] = jnp.sum(eye * y_col, axis=0, keepdims=True)
            out.append(s)
        return tuple(out)

    states = lax.fori_loop(0, tb, step, tuple(s_sc[h] for h in range(RWKV_HEADS)))
    for h in range(RWKV_HEADS):
        s_sc[h] = states[h]
        _rwkv_epilogue(y_sc[h], h, g_ref, bonus_ref, lnw_ref, lnb_ref, y_o)

    @pl.when(i == pl.num_programs(1) - 1)
    def _():
        st_o[0] = s_sc[...]


def _rwkv_scan(prep, s0, nb, t, lw):
    r, logw, kf, v, nkk, kka, g, bonus = prep
    tb = _tile(t, 128)
    nt = t // tb
    hm = pl.BlockSpec((1, RWKV_HEADS, tb, RWKV_N), lambda b, i: (b, 0, i, 0))
    st = pl.BlockSpec((1, RWKV_HEADS, RWKV_N, RWKV_N), lambda b, i: (b, 0, 0, 0))
    tokm = pl.BlockSpec((tb, RWKV_W), lambda b, i: (b * nt + i, 0))
    return pl.pallas_call(
        functools.partial(_rwkv_scan_kernel, tb=tb), grid=(nb, nt),
        in_specs=[hm] * 6 + [st, tokm, tokm, _full((1, RWKV_W)), _full((1, RWKV_W))],
        out_specs=[tokm, st],
        out_shape=[jax.ShapeDtypeStruct((nb * t, RWKV_W), F32),
                   jax.ShapeDtypeStruct((nb, RWKV_HEADS, RWKV_N, RWKV_N), F32)],
        scratch_shapes=[pltpu.VMEM((RWKV_HEADS, RWKV_N, RWKV_N), F32), pltpu.VMEM((RWKV_HEADS, tb, RWKV_N), F32)],
        compiler_params=_cparams("parallel", "arbitrary"), name="rwkv_scan",
    )(r, logw, kf, v, nkk, kka, s0, g, bonus, lw['rw_lnw'], lw['rw_lnb'])


def _rwkv_chunk_prep_kernel(r_ref, lw_ref, kf_ref, nkk_ref, kka_ref,
                            abar_o, rbar_o, bhat_o, khat_o, gam_o, aab_o, aak_o, arb_o, ark_o):
    c = RWKV_CHUNK
    ti = lax.broadcasted_iota(jnp.int32, (c, c), 0)
    si = lax.broadcasted_iota(jnp.int32, (c, c), 1)
    strict = si < ti
    incl = si <= ti
    tril = incl.astype(F32)
    for h in range(RWKV_HEADS):
        logw = lw_ref[0, h]
        cum = _dot_hi(tril, logw)
        gam = jnp.exp(cum)
        ginv = jnp.exp(-cum)
        gc = gam[c - 1:c]
        abar = nkk_ref[0, h] * jnp.exp(cum - logw)
        rbar = r_ref[0, h] * gam
        bt = kka_ref[0, h] * ginv
        kt = kf_ref[0, h] * ginv
        abar_o[0, h] = abar
        rbar_o[0, h] = rbar
        bhat_o[0, h] = bt * gc
        khat_o[0, h] = kt * gc
        gam_o[0, h] = gam
        left = jnp.concatenate([abar, rbar], axis=0).astype(BF16)
        right = jnp.concatenate([bt, kt], axis=0).astype(BF16)
        p = _dot_nt(left, right)
        aab_o[0, h] = jnp.where(strict, p[:c, :c], 0.0)
        aak_o[0, h] = jnp.where(strict, p[:c, c:], 0.0)
        arb_o[0, h] = jnp.where(incl, p[c:, :c], 0.0)
        ark_o[0, h] = jnp.where(incl, p[c:, c:], 0.0)


def _tri_inv_kernel(a_ref, t_ref):
    c, _, nb = a_ref.shape
    jrow = lax.broadcasted_iota(jnp.int32, (c, nb), 0)

    def row(i, carry):
        def inner(m, acc):
            return acc + a_ref[i, pl.ds(m, 1), :] * t_ref[m]
        t_ref[i] = lax.fori_loop(0, i, inner, (jrow == i).astype(F32))
        return carry

    lax.fori_loop(0, c, row, 0)


def _rwkv_chunk_scan_kernel(abar_ref, rbar_ref, bhat_ref, khat_ref, v_ref, gam_ref, t_ref, aak_ref, arb_ref, ark_ref,
                            s0_ref, g_ref, bonus_ref, lnw_ref, lnb_ref, y_o, st_o, s_sc):
    i = pl.program_id(1)
    c = RWKV_CHUNK

    @pl.when(i == 0)
    def _():
        s_sc[...] = s0_ref[0]

    for h in range(RWKV_HEADS):
        s = s_sc[h]
        v = v_ref[0, h]
        left = jnp.concatenate([abar_ref[0, h], rbar_ref[0, h]], axis=0).astype(BF16)
        m1 = _dot_nt(left, s.astype(BF16))
        base = m1[:c] + _dot(aak_ref[0, h].astype(BF16), v.astype(BF16))
        u = _dot(t_ref[0, h].astype(BF16), base.astype(BF16))
        uv = jnp.concatenate([u, v], axis=0).astype(BF16)
        lower = jnp.concatenate([arb_ref[0, h], ark_ref[0, h]], axis=1).astype(BF16)
        y = m1[c:] + _dot(lower, uv)
        bk = jnp.concatenate([bhat_ref[0, h], khat_ref[0, h]], axis=0).astype(BF16)
        s_sc[h] = s * gam_ref[0, h][c - 1:c] + _dot_tn(uv, bk)
        _rwkv_epilogue(y, h, g_ref, bonus_ref, lnw_ref, lnb_ref, y_o)

    @pl.when(i == pl.num_programs(1) - 1)
    def _():
        st_o[0] = s_sc[...]


def _rwkv_chunked(prep, s0, nb, t, lw):
    r, logw, kf, v, nkk, kka, g, bonus = prep
    c = RWKV_CHUNK
    nc = t // c
    hm = pl.BlockSpec((1, RWKV_HEADS, c, RWKV_N), lambda b, i: (b, 0, i, 0))
    hm_shape = jax.ShapeDtypeStruct((nb, RWKV_HEADS, t, RWKV_N), F32)
    abar, rbar, bhat, khat, gam, aab, aak, arb, ark = pl.pallas_call(
        _rwkv_chunk_prep_kernel, grid=(nb, nc), in_specs=[hm] * 5, out_specs=[hm] * 9, out_shape=[hm_shape] * 9,
        compiler_params=_cparams("parallel", "parallel"), name="rwkv_chunk_prep",
    )(r, logw, kf, nkk, kka)

    nbatch = nb * RWKV_HEADS * nc
    bn = nbatch if nbatch < 2 * LANES else 2 * LANES
    assert nbatch % bn == 0
    a_t = aab.reshape(nb, RWKV_HEADS, nc, c, c).transpose(3, 4, 0, 1, 2).reshape(c, c, nbatch)
    blk = pl.BlockSpec((c, c, bn), lambda j: (0, 0, j))
    t_t = pl.pallas_call(
        _tri_inv_kernel, grid=(nbatch // bn,), in_specs=[blk], out_specs=blk,
        out_shape=jax.ShapeDtypeStruct((c, c, nbatch), F32),
        compiler_params=_cparams("parallel"), name="rwkv_tri_inv",
    )(a_t)
    tinv = t_t.reshape(c, c, nb, RWKV_HEADS, nc).transpose(2, 3, 4, 0, 1).reshape(nb, RWKV_HEADS, t, c)

    st = pl.BlockSpec((1, RWKV_HEADS, RWKV_N, RWKV_N), lambda b, i: (b, 0, 0, 0))
    tokm = pl.BlockSpec((c, RWKV_W), lambda b, i: (b * nc + i, 0))
    return pl.pallas_call(
        _rwkv_chunk_scan_kernel, grid=(nb, nc),
        in_specs=[hm] * 10 + [st, tokm, tokm, _full((1, RWKV_W)), _full((1, RWKV_W))],
        out_specs=[tokm, st],
        out_shape=[jax.ShapeDtypeStruct((nb * t, RWKV_W), F32),
                   jax.ShapeDtypeStruct((nb, RWKV_HEADS, RWKV_N, RWKV_N), F32)],
        scratch_shapes=[pltpu.VMEM((RWKV_HEADS, RWKV_N, RWKV_N), F32)],
        compiler_params=_cparams("parallel", "arbitrary"), name="rwkv_chunk_scan",
    )(abar, rbar, bhat, khat, v, gam, tinv, aak, arb, ark, s0, g, bonus, lw['rw_lnw'], lw['rw_lnb'])


def _rwkv_mix(prep, s0, nb, t, lw):
    if t % RWKV_CHUNK == 0:
        return _rwkv_chunked(prep, s0, nb, t, lw)
    return _rwkv_scan(prep, s0, nb, t, lw)


def _diff_lambda(lam_ref, lam_init):
    lp = lam_ref[...]
    s1 = jnp.sum(lp[0:1] * lp[1:2], axis=-1, keepdims=True)
    s2 = jnp.sum(lp[2:3] * lp[3:4], axis=-1, keepdims=True)
    return jnp.exp(s1) - jnp.exp(s2) + lam_init


def _flash_kernel(qi_tab, ki_tab, *refs, mode, tq, lam_init):
    if mode == 'diff':
        q_ref, k_ref, v_ref, lam_ref, g_ref, o_ref, m_sc, l_sc, acc_sc = refs
        pid = pl.program_id(2)
        n_maps = 2
    else:
        q_ref, kv_ref, wuv_ref, o_ref, m_sc, l_sc, acc_sc = refs
        pid = pl.program_id(1)
        n_maps = 1
    qi = qi_tab[pid]
    ki = ki_tab[pid]
    rows = m_sc.shape[1]

    @pl.when(ki == 0)
    def _():
        m_sc[...] = jnp.full(m_sc.shape, NEG_BIG, F32)
        l_sc[...] = jnp.zeros(l_sc.shape, F32)
        acc_sc[...] = jnp.zeros(acc_sc.shape, F32)

    def update(masked):
        for m in range(n_maps):
            if mode == 'diff':
                q = q_ref[2 * m:2 * m + 2].reshape(rows, DIFF_DK)
                k = k_ref[m]
                v = v_ref[...]
            else:
                q = q_ref[...].reshape(rows, MLA_CACHE_W)
                k = kv_ref[...]
                v = k[:, :MLA_KV_LORA]
            s = _dot_nt(q, k)
            if masked:
                qpos = lax.broadcasted_iota(jnp.int32, s.shape, 0) % tq
                kpos = lax.broadcasted_iota(jnp.int32, s.shape, 1)
                s = jnp.where(kpos <= qpos, s, NEG_BIG)
            m_prev = m_sc[m]
            m_new = jnp.maximum(m_prev, jnp.max(s, axis=-1, keepdims=True))
            alpha = jnp.exp(m_prev - m_new)
            p = jnp.exp(s - _rep(m_new, s.shape[1] // LANES))
            l_sc[m] = alpha * l_sc[m] + jnp.sum(p, axis=-1, keepdims=True)
            acc_sc[m] = alpha * acc_sc[m] + _dot(p.astype(BF16), v)
            m_sc[m] = m_new

    @pl.when(ki < qi)
    def _():
        update(False)

    @pl.when(ki == qi)
    def _():
        update(True)
        if mode == 'diff':
            lam = _diff_lambda(lam_ref, lam_init)
            o = acc_sc[0] / l_sc[0] - lam * (acc_sc[1] / l_sc[1])
            y = _rms(o, g_ref[...], SUBLN_EPS) * (1.0 - lam_init)
            for r in range(2):
                o_ref[:, r * DIFF_DV:(r + 1) * DIFF_DV] = y[r * tq:(r + 1) * tq]
        else:
            o = (acc_sc[0] / l_sc[0]).astype(BF16)
            for h in range(MLA_HEADS):
                o_ref[:, h * MLA_V:(h + 1) * MLA_V] = _dot(o[h * tq:(h + 1) * tq], wuv_ref[h])


def _pair_tables(nq):
    qi = [q for q in range(nq) for _ in range(q + 1)]
    ki = [k for q in range(nq) for k in range(q + 1)]
    return jnp.asarray(qi, jnp.int32), jnp.asarray(ki, jnp.int32)


def _prompt_diff_attn(qd, kh, vbf, nb, t, lw, lam_init):
    tq = _tile(t, 512)
    nq = t // tq
    qi_tab, ki_tab = _pair_tables(nq)
    rows = 2 * tq
    grid_spec = pltpu.PrefetchScalarGridSpec(
        num_scalar_prefetch=2, grid=(nb, 2, len(qi_tab)),
        in_specs=[
            pl.BlockSpec((4, tq, DIFF_DK), lambda b, h, p, qt, kt: (h, b * nq + qt[p], 0)),
            pl.BlockSpec((2, tq, DIFF_DK), lambda b, h, p, qt, kt: (h, b * nq + kt[p], 0)),
            pl.BlockSpec((tq, DIFF_DV), lambda b, h, p, qt, kt: (b * nq + kt[p], h)),
            pl.BlockSpec((4, DIFF_DK), lambda b, h, p, qt, kt: (0, 0)),
            pl.BlockSpec((1, DIFF_DV), lambda b, h, p, qt, kt: (0, 0)),
        ],
        out_specs=pl.BlockSpec((tq, 2 * DIFF_DV), lambda b, h, p, qt, kt: (b * nq + qt[p], h)),
        scratch_shapes=[pltpu.VMEM((2, rows, LANES), F32), pltpu.VMEM((2, rows, LANES), F32),
                        pltpu.VMEM((2, rows, DIFF_DV), F32)],
    )
    return pl.pallas_call(
        functools.partial(_flash_kernel, mode='diff', tq=tq, lam_init=lam_init), grid_spec=grid_spec,
        out_shape=jax.ShapeDtypeStruct((nb * t, 4 * DIFF_DV), F32),
        compiler_params=_cparams("parallel", "parallel", "arbitrary"), name="prompt_diff_attn",
    )(qi_tab, ki_tab, qd, kh, vbf, lw['diff_lambda'], lw['diff_subln'])


def _prompt_mla_attn(qm, kvbf, nb, t, lw):
    tq = _tile(t, 256)
    nq = t // tq
    qi_tab, ki_tab = _pair_tables(nq)
    rows = MLA_HEADS * tq
    grid_spec = pltpu.PrefetchScalarGridSpec(
        num_scalar_prefetch=2, grid=(nb, len(qi_tab)),
        in_specs=[
            pl.BlockSpec((MLA_HEADS, tq, MLA_CACHE_W), lambda b, p, qt, kt: (0, b * nq + qt[p], 0)),
            pl.BlockSpec((tq, MLA_CACHE_W), lambda b, p, qt, kt: (b * nq + kt[p], 0)),
            pl.BlockSpec((MLA_HEADS, MLA_KV_LORA, MLA_V), lambda b, p, qt, kt: (0, 0, 0)),
        ],
        out_specs=pl.BlockSpec((tq, MLA_HEADS * MLA_V), lambda b, p, qt, kt: (b * nq + qt[p], 0)),
        scratch_shapes=[pltpu.VMEM((1, rows, LANES), F32), pltpu.VMEM((1, rows, LANES), F32),
                        pltpu.VMEM((1, rows, MLA_KV_LORA), F32)],
    )
    return pl.pallas_call(
        functools.partial(_flash_kernel, mode='mla', tq=tq, lam_init=0.0), grid_spec=grid_spec,
        out_shape=jax.ShapeDtypeStruct((nb * t, MLA_HEADS * MLA_V), F32),
        compiler_params=_cparams("parallel", "arbitrary"), name="prompt_mla_attn",
    )(qi_tab, ki_tab, qm, kvbf, lw['wuv'])


def _decode_kernel(pt_ref, *refs, mode, npg, tn, lam_init):
    del pt_ref
    q_ref = refs[0]
    k_refs = refs[1:1 + npg]
    if mode == 'diff':
        v_refs = refs[1 + npg:1 + 2 * npg]
        kn_ref, vn_ref, lam_ref, g_ref, o_ref, m_sc, l_sc, acc_sc = refs[1 + 2 * npg:]
        groups = 4
    else:
        kn_ref, wuv_ref, o_ref, m_sc, l_sc, acc_sc = refs[1 + npg:]
        groups = 1
    c = pl.program_id(1)
    rows = q_ref.shape[1]
    gr = rows // groups

    @pl.when(c == 0)
    def _():
        m_sc[...] = jnp.full(m_sc.shape, NEG_BIG, F32)
        l_sc[...] = jnp.zeros(l_sc.shape, F32)
        acc_sc[...] = jnp.zeros(acc_sc.shape, F32)

    q = q_ref[0]
    if mode == 'diff':
        parts = []
        for h in range(2):
            for m in range(2):
                kcat = jnp.concatenate([kr[0, 0, h, m] for kr in k_refs], axis=1).astype(BF16)
                g = h * 2 + m
                parts.append(_dot(q[g * gr:(g + 1) * gr], kcat))
        s = jnp.concatenate(parts, axis=0)
    else:
        kcat = jnp.concatenate([kr[0, 0] for kr in k_refs], axis=1).astype(BF16)
        s = _dot(q, kcat)
    m_prev = m_sc[...]
    m_new = jnp.maximum(m_prev, jnp.max(s, axis=-1, keepdims=True))
    alpha = jnp.exp(m_prev - m_new)
    p = jnp.exp(s - m_new).astype(BF16)
    l_sc[...] = alpha * l_sc[...] + jnp.sum(p.astype(F32), axis=-1, keepdims=True)
    if mode == 'diff':
        pv = jnp.concatenate(
            [_dot(p[h * 2 * gr:(h + 1) * 2 * gr],
                  jnp.concatenate([vr[0, 0, :, h, :] for vr in v_refs], axis=0).astype(BF16)) for h in range(2)], axis=0)
    else:
        pv = _dot_nt(p, kcat[:MLA_KV_LORA])
    acc_sc[...] = alpha * acc_sc[...] + pv
    m_sc[...] = m_new

    @pl.when(c == pl.num_programs(1) - 1)
    def _():
        qf = q.astype(F32)
        kn = kn_ref[0].astype(BF16).astype(F32)
        vn = (vn_ref[0] if mode == 'diff' else kn_ref[0][:, :MLA_KV_LORA]).astype(BF16).astype(F32)
        tok = lax.broadcasted_iota(jnp.int32, (rows, 1), 0) % tn
        sn = []
        for j in range(tn):
            if mode == 'diff':
                dots = [jnp.sum(qf[g * gr:(g + 1) * gr] * kn[j:j + 1, g * DIFF_DK:(g + 1) * DIFF_DK],
                                axis=-1, keepdims=True) for g in range(groups)]
                sj = jnp.concatenate(dots, axis=0)
            else:
                sj = jnp.sum(qf * kn[j:j + 1], axis=-1, keepdims=True)
            sn.append(jnp.where(tok >= j, sj, NEG_BIG))
        m_prev = m_sc[...]
        m_new = m_prev
        for sj in sn:
            m_new = jnp.maximum(m_new, sj)
        alpha = jnp.exp(m_prev - m_new)
        l = alpha * l_sc[...]
        acc = alpha * acc_sc[...]
        for j, sj in enumerate(sn):
            pj = jnp.exp(sj - m_new)
            l = l + pj
            if mode == 'diff':
                vrow = jnp.concatenate(
                    [jnp.broadcast_to(vn[j:j + 1, h * DIFF_DV:(h + 1) * DIFF_DV], (2 * gr, DIFF_DV)) for h in range(2)],
                    axis=0)
            else:
                vrow = vn[j:j + 1]
            acc = acc + pj * vrow
        o = acc / l
        if mode == 'diff':
            lam = _diff_lambda(lam_ref, lam_init)
            for h in range(2):
                base = h * 2 * gr
                oh = o[base:base + gr] - lam * o[base + gr:base + 2 * gr]
                y = _rms(oh, g_ref[...], SUBLN_EPS) * (1.0 - lam_init)
                for r in range(2):
                    col = (h * 2 + r) * DIFF_DV
                    o_ref[0, :, col:col + DIFF_DV] = y[r * tn:(r + 1) * tn]
        else:
            ob = o.astype(BF16)
            for h in range(MLA_HEADS):
                o_ref[0, :, h * MLA_V:(h + 1) * MLA_V] = _dot(ob[h * tn:(h + 1) * tn], wuv_ref[h])


def _sample_attn(mode, layer, q, cache_k, cache_v, knew, vnew, page_table, lw, lam_init):
    nb, rows, dk = q.shape
    n_pages = page_table.shape[1]
    npg = min(PAGES_PER_STEP, n_pages)
    assert n_pages % npg == 0
    tn = knew.shape[1]

    def page_spec(cache, j):
        zeros = (0,) * (cache.ndim - 2)
        return pl.BlockSpec((1, 1) + cache.shape[2:], lambda b, c, pt: (layer, pt[b, c * npg + j]) + zeros)

    seq3 = lambda b, c, pt: (b, 0, 0)
    in_specs = [pl.BlockSpec((1, rows, dk), seq3)] + [page_spec(cache_k, j) for j in range(npg)]
    operands = [q] + [cache_k] * npg
    if mode == 'diff':
        in_specs += [page_spec(cache_v, j) for j in range(npg)]
        operands += [cache_v] * npg
        in_specs += [pl.BlockSpec((1, tn, knew.shape[-1]), seq3), pl.BlockSpec((1, tn, vnew.shape[-1]), seq3),
                     pl.BlockSpec((4, DIFF_DK), lambda b, c, pt: (0, 0)),
                     pl.BlockSpec((1, DIFF_DV), lambda b, c, pt: (0, 0))]
        operands += [knew, vnew, lw['diff_lambda'], lw['diff_subln']]
        dv, width = DIFF_DV, 4 * DIFF_DV
    else:
        in_specs += [pl.BlockSpec((1, tn, knew.shape[-1]), seq3),
                     pl.BlockSpec((MLA_HEADS, MLA_KV_LORA, MLA_V), lambda b, c, pt: (0, 0, 0))]
        operands += [knew, lw['wuv']]
        dv, width = MLA_KV_LORA, MLA_HEADS * MLA_V
    grid_spec = pltpu.PrefetchScalarGridSpec(
        num_scalar_prefetch=1, grid=(nb, n_pages // npg), in_specs=in_specs,
        out_specs=pl.BlockSpec((1, tn, width), seq3),
        scratch_shapes=[pltpu.VMEM((rows, 1), F32), pltpu.VMEM((rows, 1), F32), pltpu.VMEM((rows, dv), F32)],
    )
    return pl.pallas_call(
        functools.partial(_decode_kernel, mode=mode, npg=npg, tn=tn, lam_init=lam_init), grid_spec=grid_spec,
        out_shape=jax.ShapeDtypeStruct((nb, tn, width), F32),
        compiler_params=_cparams("parallel", "arbitrary"), name="sample_%s_attn" % mode,
    )(page_table, *operands)


def _mix_out_kernel(x_ref, yrw_ref, yd_ref, ym_ref, w1, w2, w3, gx_ref, wxq, h_o, qx_o):
    h = (x_ref[...] + _dot(yrw_ref[...].astype(BF16), w1[...]) + _dot(yd_ref[...].astype(BF16), w2[...])
         + _dot(ym_ref[...].astype(BF16), w3[...]))
    h_o[...] = h
    qx_o[...] = _dot(_rms(h, gx_ref[...], RMS_EPS).astype(BF16), wxq[...])


def _mix_out(x, yrw, yd, ym, lw):
    n = x.shape[0]
    tm = _tile(n, 512)
    row = lambda w: pl.BlockSpec((tm, w), lambda i: (i, 0))
    ws = [lw['w_out_rw'], lw['w_out_d'], lw['w_out_m'], lw['g_cross'], lw['w_xq']]
    return pl.pallas_call(
        _mix_out_kernel, grid=(n // tm,),
        in_specs=[row(D_MODEL), row(RWKV_W), row(4 * DIFF_DV), row(MLA_HEADS * MLA_V)] + [_full(w.shape) for w in ws],
        out_specs=[row(D_MODEL), row(MEM_W)],
        out_shape=[jax.ShapeDtypeStruct((n, D_MODEL), F32), jax.ShapeDtypeStruct((n, MEM_W), F32)],
        compiler_params=_cparams("parallel"), name="mix_out",
    )(x, yrw, yd, ym, *ws)


def _mem_kv_kernel(x_ref, g_ref, wk, wv, k_o, v_o):
    n = _rms(x_ref[...], g_ref[...], RMS_EPS).astype(BF16)
    k_o[...] = _dot(n, wk[...])
    v_o[...] = _dot(n, wv[...])


def _mem_kv(mem, lw):
    n = mem.shape[0]
    tm = _tile(n, 512)
    row = lambda w: pl.BlockSpec((tm, w), lambda i: (i, 0))
    return pl.pallas_call(
        _mem_kv_kernel, grid=(n // tm,),
        in_specs=[row(D_MODEL), _full((1, D_MODEL)), _full(lw['w_xk'].shape), _full(lw['w_xv'].shape)],
        out_specs=[row(MEM_W), row(MEM_W)],
        out_shape=[jax.ShapeDtypeStruct((n, MEM_W), F32)] * 2,
        compiler_params=_cparams("parallel"), name="mem_kv",
    )(mem, lw['g_mem'], lw['w_xk'], lw['w_xv'])


def _cross_kernel(q_ref, k_ref, v_ref, o_ref):
    q = q_ref[...].astype(BF16)
    k = k_ref[0, 0].astype(BF16)
    v = v_ref[0, 0].astype(BF16)
    for h in range(MEM_HEADS):
        sl = slice(h * MEM_HEAD_DIM, (h + 1) * MEM_HEAD_DIM)
        s = _dot_nt(q[:, sl], k[:, sl])
        p = jnp.exp(s - jnp.max(s, axis=-1, keepdims=True))
        p = p / jnp.sum(p, axis=-1, keepdims=True)
        o_ref[:, sl] = _dot(p.astype(BF16), v[:, sl])


def _cross_attn(qx, row0, nseq, t, mk, mv, layer):
    tq = _tile(t, 512)
    nt = t // tq
    blk0 = row0 // tq
    mem = pl.BlockSpec((1, 1) + mk.shape[2:], lambda s, i: (layer, s, 0, 0))
    return pl.pallas_call(
        _cross_kernel, grid=(nseq, nt),
        in_specs=[pl.BlockSpec((tq, MEM_W), lambda s, i: (blk0 + s * nt + i, 0)), mem, mem],
        out_specs=pl.BlockSpec((tq, MEM_W), lambda s, i: (s * nt + i, 0)),
        out_shape=jax.ShapeDtypeStruct((nseq * t, MEM_W), F32),
        compiler_params=_cparams("parallel", "arbitrary"), name="cross_attn",
    )(qx, mk, mv)


def _ffn_kernel(h_ref, ox_ref, wxo, g_ref, wg, wu, wd, o_ref, h2_sc, n_sc, acc_sc):
    f = pl.program_id(1)

    @pl.when(f == 0)
    def _():
        h2 = h_ref[...] + _dot(ox_ref[...].astype(BF16), wxo[...])
        h2_sc[...] = h2
        n_sc[...] = _rms(h2, g_ref[...], RMS_EPS).astype(BF16)
        acc_sc[...] = jnp.zeros(acc_sc.shape, F32)

    n = n_sc[...]
    hg = _dot(n, wg[...])
    acc_sc[...] += _dot((jax.nn.silu(hg) * _dot(n, wu[...])).astype(BF16), wd[...])

    @pl.when(f == pl.num_programs(1) - 1)
    def _():
        o_ref[...] = h2_sc[...] + acc_sc[...]


def _ffn_dense(h, ox, lw):
    n = h.shape[0]
    tm = _tile(n, 1024)
    dff = lw['ffn_wg'].shape[1]
    tf = 256
    assert dff % tf == 0
    row = lambda w: pl.BlockSpec((tm, w), lambda i, f: (i, 0))
    return pl.pallas_call(
        _ffn_kernel, grid=(n // tm, dff // tf),
        in_specs=[row(D_MODEL), row(MEM_W), pl.BlockSpec((MEM_W, D_MODEL), lambda i, f: (0, 0)),
                  pl.BlockSpec((1, D_MODEL), lambda i, f: (0, 0)),
                  pl.BlockSpec((D_MODEL, tf), lambda i, f: (0, f)), pl.BlockSpec((D_MODEL, tf), lambda i, f: (0, f)),
                  pl.BlockSpec((tf, D_MODEL), lambda i, f: (f, 0))],
        out_specs=row(D_MODEL), out_shape=jax.ShapeDtypeStruct((n, D_MODEL), F32),
        scratch_shapes=[pltpu.VMEM((tm, D_MODEL), F32), pltpu.VMEM((tm, D_MODEL), BF16),
                        pltpu.VMEM((tm, D_MODEL), F32)],
        compiler_params=_cparams("parallel", "arbitrary"), name="ffn_dense",
    )(h, ox, lw['w_xo'], lw['g_ffn'], lw['ffn_wg'], lw['ffn_wu'], lw['ffn_wd'])


def _router_kernel(h_ref, ox_ref, wxo, g_ref, wr, h2_o, n_o, comb_o, pos_o, cnt_o):
    h2 = h_ref[...] + _dot(ox_ref[...].astype(BF16), wxo[...])
    h2_o[...] = h2
    nf = _rms(h2, g_ref[...], RMS_EPS)
    n_o[...] = nf.astype(BF16)
    logits = _dot_hi(nf, wr[...])
    tm = logits.shape[0]
    eid = lax.broadcasted_iota(jnp.int32, logits.shape, 1)
    m1 = jnp.max(logits, axis=-1, keepdims=True)
    i1 = jnp.min(jnp.where(logits == m1, eid, N_EXPERTS), axis=-1, keepdims=True)
    sel1 = eid == i1
    rest = jnp.where(sel1, -jnp.inf, logits)
    m2 = jnp.max(rest, axis=-1, keepdims=True)
    i2 = jnp.min(jnp.where(rest == m2, eid, N_EXPERTS), axis=-1, keepdims=True)
    sel2 = eid == i2
    e2 = jnp.exp(m2 - m1)
    den = 1.0 + e2
    comb_o[...] = jnp.where(sel1, 1.0 / den, 0.0) + jnp.where(sel2, e2 / den, 0.0)
    sel = jnp.logical_or(sel1, sel2).astype(F32)
    lower = (lax.broadcasted_iota(jnp.int32, (tm, tm), 1) < lax.broadcasted_iota(jnp.int32, (tm, tm), 0))
    rank = _dot(lower.astype(BF16), sel.astype(BF16))
    pos_o[...] = jnp.where(sel > 0.0, rank, -1.0)
    cnt_o[0] = jnp.sum(sel, axis=0, keepdims=True)


def _router(h, ox, lw, tm):
    n = h.shape[0]
    row = lambda w: pl.BlockSpec((tm, w), lambda i: (i, 0))
    return pl.pallas_call(
        _router_kernel, grid=(n // tm,),
        in_specs=[row(D_MODEL), row(MEM_W), _full((MEM_W, D_MODEL)), _full((1, D_MODEL)),
                  _full((D_MODEL, N_EXPERTS))],
        out_specs=[row(D_MODEL), row(D_MODEL), row(N_EXPERTS), row(N_EXPERTS),
                   pl.BlockSpec((1, 1, N_EXPERTS), lambda i: (i, 0, 0))],
        out_shape=[jax.ShapeDtypeStruct((n, D_MODEL), F32), jax.ShapeDtypeStruct((n, D_MODEL), BF16),
                   jax.ShapeDtypeStruct((n, N_EXPERTS), F32), jax.ShapeDtypeStruct((n, N_EXPERTS), F32),
                   jax.ShapeDtypeStruct((n // tm, 1, N_EXPERTS), F32)],
        compiler_params=_cparams("parallel"), name="moe_router",
    )(h, ox, lw['w_xo'], lw['g_ffn'], lw['moe_router'])


def _moe_kernel(cnt_ref, n_ref, post_ref, pos_ref, comb_ref, h2_ref, wg, wu, wd, o_ref, xe_sc, acc_sc):
    i = pl.program_id(0)
    e = pl.program_id(1)
    f = pl.program_id(2)
    tm = n_ref.shape[0]
    ch = MOE_CHUNK
    nch = (cnt_ref[i * N_EXPERTS + e] + ch - 1) // ch

    @pl.when(jnp.logical_and(e == 0, f == 0))
    def _():
        o_ref[...] = h2_ref[...]

    @pl.when(f == 0)
    def _():
        esub = lax.broadcasted_iota(jnp.int32, (N_EXPERTS, tm), 0)
        pos_row = jnp.sum(jnp.where(esub == e, post_ref[...], 0.0), axis=0, keepdims=True)
        slot = lax.broadcasted_iota(jnp.int32, (ch, tm), 0).astype(F32)

        def gather(c, carry):
            base = pl.multiple_of(c * ch, ch)
            onehot = (pos_row == slot + (c * ch).astype(F32)).astype(BF16)
            xe_sc[pl.ds(base, ch), :] = _dot(onehot, n_ref[...]).astype(BF16)
            acc_sc[pl.ds(base, ch), :] = jnp.zeros((ch, D_MODEL), F32)
            return carry

        lax.fori_loop(0, nch, gather, 0)

    def expert(c, carry):
        base = pl.multiple_of(c * ch, ch)
        x = xe_sc[pl.ds(base, ch), :]
        hh = (jax.nn.silu(_dot(x, wg[0])) * _dot(x, wu[0])).astype(BF16)
        acc_sc[pl.ds(base, ch), :] += _dot(hh, wd[0])
        return carry

    lax.fori_loop(0, nch, expert, 0)

    @pl.when(f == pl.num_programs(2) - 1)
    def _():
        elane = lax.broadcasted_iota(jnp.int32, (tm, N_EXPERTS), 1)
        pos_col = jnp.sum(jnp.where(elane == e, pos_ref[...], 0.0), axis=-1, keepdims=True)
        gate_col = jnp.sum(jnp.where(elane == e, comb_ref[...], 0.0), axis=-1, keepdims=True)
        slot = lax.broadcasted_iota(jnp.int32, (tm, ch), 1).astype(F32)

        def scatter(c, carry):
            base = pl.multiple_of(c * ch, ch)
            onehot = (pos_col == slot + (c * ch).astype(F32)).astype(BF16)
            y = acc_sc[pl.ds(base, ch), :].astype(BF16)
            o_ref[...] += gate_col * _dot(onehot, y)
            return carry

        lax.fori_loop(0, nch, scatter, 0)


def _moe(h, ox, lw):
    n = h.shape[0]
    tm = _tile(n, 1024)
    h2, nb, comb, pos, cnt = _router(h, ox, lw, tm)
    pos_t = pos.T
    cnt = cnt.reshape(-1).astype(jnp.int32)
    dff = lw['moe_wg'].shape[2]
    tf = 512
    assert dff % tf == 0
    tok = lambda w: pl.BlockSpec((tm, w), lambda i, e, f, c: (i, 0))
    grid_spec = pltpu.PrefetchScalarGridSpec(
        num_scalar_prefetch=1, grid=(n // tm, N_EXPERTS, dff // tf),
        in_specs=[tok(D_MODEL), pl.BlockSpec((N_EXPERTS, tm), lambda i, e, f, c: (0, i)),
                  tok(N_EXPERTS), tok(N_EXPERTS), tok(D_MODEL),
                  pl.BlockSpec((1, D_MODEL, tf), lambda i, e, f, c: (e, 0, f)),
                  pl.BlockSpec((1, D_MODEL, tf), lambda i, e, f, c: (e, 0, f)),
                  pl.BlockSpec((1, tf, D_MODEL), lambda i, e, f, c: (e, f, 0))],
        out_specs=tok(D_MODEL),
        scratch_shapes=[pltpu.VMEM((tm, D_MODEL), BF16), pltpu.VMEM((tm, D_MODEL), F32)],
    )
    return pl.pallas_call(
        _moe_kernel, grid_spec=grid_spec, out_shape=jax.ShapeDtypeStruct((n, D_MODEL), F32),
        compiler_params=_cparams("parallel", "arbitrary", "arbitrary"), name="moe_experts",
    )(cnt, nb, pos_t, pos, comb, h2, lw['moe_wg'], lw['moe_wu'], lw['moe_wd'])


def _final_norm_kernel(x_ref, g_ref, o_ref):
    o_ref[...] = _rms(x_ref[...], g_ref[...], RMS_EPS)


def _final_norm(x, row0, rows, g):
    tm = _tile(rows, 1024)
    blk0 = row0 // tm
    return pl.pallas_call(
        _final_norm_kernel, grid=(rows // tm,),
        in_specs=[pl.BlockSpec((tm, D_MODEL), lambda i: (blk0 + i, 0)), _full((1, D_MODEL))],
        out_specs=pl.BlockSpec((tm, D_MODEL), lambda i: (i, 0)),
        out_shape=jax.ShapeDtypeStruct((rows, D_MODEL), F32),
        compiler_params=_cparams("parallel"), name="final_norm",
    )(x, g)


def _rot_cols(w):
    half = w.shape[-1] // 2
    return jnp.concatenate([-w[..., half:], w[..., :half]], axis=-1)


def _layer_weights(p, l):
    bf = lambda a: a.astype(BF16)
    r2 = lambda a: a.reshape(1, -1).astype(F32)
    w_in = p['w_in'][l]
    c1 = RWKV_IN
    c2 = c1 + DIFF_Q_W
    c3 = c2 + DIFF_K_W
    c4 = c3 + DIFF_V_W
    c5 = c4 + MLA_Q_LORA
    c6 = c5 + MLA_KV_LORA
    wq = w_in[:, c1:c2].reshape(D_MODEL, 2, 2, 2, DIFF_DK).transpose(0, 1, 3, 2, 4).reshape(D_MODEL, DIFF_Q_W)
    wuq = p['mla_wuq'][l]
    w_out = p['w_out'][l]
    lw = dict(
        g_mix=r2(p['norm_mix'][l]), w_rw=bf(w_in[:, :c1]), w_q=bf(wq * DIFF_DK ** -0.5), w_k=bf(w_in[:, c2:c3]),
        w_v=bf(w_in[:, c3:c4]), w_cq=bf(w_in[:, c4:c5]), w_ckv=bf(w_in[:, c5:c6]), w_kr=bf(w_in[:, c6:]),
        w_krr=bf(_rot_cols(w_in[:, c6:])), g_q=r2(p['mla_qnorm'][l]), g_kv=r2(p['mla_kvnorm'][l]),
        wq_nope=bf(wuq[:, :, :MLA_NOPE].transpose(1, 0, 2)), wq_rope=bf(wuq[:, :, MLA_NOPE:].transpose(1, 0, 2)),
        wq_rope_rot=bf(_rot_cols(wuq[:, :, MLA_NOPE:]).transpose(1, 0, 2)),
        wuk_t=bf(p['mla_wuk'][l].transpose(1, 2, 0)), wuv=bf(p['mla_wuv'][l].transpose(1, 0, 2)),
        rw_mu=r2(p['rwkv_mu'][l]), rw_w0=r2(p['rwkv_w0'][l]), rw_a0=r2(p['rwkv_a0'][l]), rw_kk=r2(p['rwkv_kk'][l]),
        rw_ka=r2(p['rwkv_ka'][l]), rw_rk=r2(p['rwkv_rk'][l]), rw_w2=bf(p['rwkv_w2'][l]), rw_a2=bf(p['rwkv_a2'][l]),
        rw_g2=bf(p['rwkv_g2'][l]), rw_lnw=r2(p['rwkv_lnw'][l]), rw_lnb=r2(p['rwkv_lnb'][l]),
        bones=jnp.kron(jnp.eye(RWKV_HEADS, dtype=F32), jnp.ones((RWKV_N, RWKV_N), F32)),
        diff_lambda=p['diff_lambda'][l].astype(F32), diff_subln=r2(p['diff_subln'][l]),
        w_out_rw=bf(w_out[:RWKV_W]), w_out_d=bf(w_out[RWKV_W:RWKV_W + 4 * DIFF_DV]),
        w_out_m=bf(w_out[RWKV_W + 4 * DIFF_DV:]), g_cross=r2(p['norm_cross'][l]),
        w_xq=bf(p['w_xq'][l] * MEM_HEAD_DIM ** -0.5), g_mem=r2(p['norm_mem'][l]), w_xk=bf(p['w_xk'][l]),
        w_xv=bf(p['w_xv'][l]), w_xo=bf(p['w_xo'][l]), g_ffn=r2(p['norm_ffn'][l]),
    )
    i = l // 2
    if l % 2 == 0:
        lw.update(ffn_wg=bf(p['ffn_wg'][i]), ffn_wu=bf(p['ffn_wu'][i]), ffn_wd=bf(p['ffn_wd'][i]))
    else:
        lw.update(moe_router=p['moe_router'][i].astype(F32), moe_wg=bf(p['moe_wg'][i]), moe_wu=bf(p['moe_wu'][i]),
                  moe_wd=bf(p['moe_wd'][i]))
    return lw


def _rope_tables(pos):
    half = MLA_ROPE // 2
    inv = ROPE_THETA ** (-jnp.arange(half, dtype=F32) / half)
    ang = pos.astype(F32)[:, None] * inv[None, :]
    cos, sin = jnp.cos(ang), jnp.sin(ang)
    return jnp.concatenate([cos, cos], axis=-1), jnp.concatenate([sin, sin], axis=-1)


def kernel(x_prompt, x_sample, state_rwkv, state_shift, cache_diff_k, cache_diff_v, cache_mla_kv, cache_mem_k, cache_mem_v, page_table, mem_prompt, norm_mix, w_in, rwkv_mu, rwkv_w0, rwkv_w2, rwkv_a0, rwkv_a2, rwkv_g2, rwkv_kk, rwkv_ka, rwkv_rk, rwkv_lnw, rwkv_lnb, diff_lambda, diff_subln, mla_qnorm, mla_kvnorm, mla_wuq, mla_wuk, mla_wuv, w_out, norm_cross, norm_mem, w_xq, w_xk, w_xv, w_xo, norm_ffn, ffn_wg, ffn_wu, ffn_wd, moe_router, moe_wg, moe_wu, moe_wd, final_norm):
    p = dict(norm_mix=norm_mix, w_in=w_in, rwkv_mu=rwkv_mu, rwkv_w0=rwkv_w0, rwkv_w2=rwkv_w2, rwkv_a0=rwkv_a0,
             rwkv_a2=rwkv_a2, rwkv_g2=rwkv_g2, rwkv_kk=rwkv_kk, rwkv_ka=rwkv_ka, rwkv_rk=rwkv_rk,
             rwkv_lnw=rwkv_lnw, rwkv_lnb=rwkv_lnb, diff_lambda=diff_lambda, diff_subln=diff_subln,
             mla_qnorm=mla_qnorm, mla_kvnorm=mla_kvnorm, mla_wuq=mla_wuq, mla_wuk=mla_wuk, mla_wuv=mla_wuv,
             w_out=w_out, norm_cross=norm_cross, norm_mem=norm_mem, w_xq=w_xq, w_xk=w_xk, w_xv=w_xv, w_xo=w_xo,
             norm_ffn=norm_ffn, ffn_wg=ffn_wg, ffn_wu=ffn_wu, ffn_wd=ffn_wd, moe_router=moe_router, moe_wg=moe_wg,
             moe_wu=moe_wu, moe_wd=moe_wd)
    bp, t = x_prompt.shape[:2]
    bs, tn = x_sample.shape[:2]
    depth = w_in.shape[0]
    n_p, n_s = bp * t, bs * tn
    past = page_table.shape[1] * PAGE_SIZE
    mem_len = mem_prompt.shape[1]

    x = jnp.concatenate([x_prompt.reshape(n_p, D_MODEL), x_sample.reshape(n_s, D_MODEL)], axis=0)
    pos = jnp.concatenate([jnp.tile(jnp.arange(t), bp), jnp.tile(past + jnp.arange(tn), bs)])
    cos, sin = _rope_tables(pos)
    ck_t = cache_diff_k.transpose(0, 1, 3, 4, 5, 2)
    ckv_t = cache_mla_kv.transpose(0, 1, 3, 2)
    cmk = cache_mem_k.reshape(depth, bs, mem_len, MEM_W)
    cmv = cache_mem_v.reshape(depth, bs, mem_len, MEM_W)
    mem = mem_prompt.reshape(bp * mem_len, D_MODEL)
    zeros_shift = jnp.zeros((bp, RWKV_IN), F32)
    zeros_state = jnp.zeros((bp, RWKV_HEADS, RWKV_N, RWKV_N), F32)

    outs = {k: [] for k in ('pS', 'psh', 'pdk', 'pdv', 'pkv', 'pmk', 'pmv', 'sS', 'ssh', 'sdk', 'sdv', 'skv')}
    for l in range(depth):
        lw = _layer_weights(p, l)
        lam_init = 0.8 - 0.6 * math.exp(-0.3 * l)
        xrw, qd, ktok, kh, vtok, vbf, kv, kvbf, qm = _mixer_in(x, lw, cos, sin)

        yrw_p, st_p = _rwkv_mix(_rwkv_prep(xrw, 0, bp, t, zeros_shift, lw), zeros_state, bp, t, lw)
        yrw_s, st_s = _rwkv_mix(_rwkv_prep(xrw, n_p, bs, tn, state_shift[l], lw), state_rwkv[l], bs, tn, lw)

        yd_p = _prompt_diff_attn(qd, kh, vbf, bp, t, lw, lam_init)
        qd_s = qd[:, n_p:].reshape(8, bs, tn, DIFF_DK).transpose(1, 0, 2, 3).reshape(bs, 8 * tn, DIFF_DK)
        yd_s = _sample_attn('diff', l, qd_s, ck_t, cache_diff_v,
                            ktok[n_p:].reshape(bs, tn, DIFF_K_W), vtok[n_p:].reshape(bs, tn, DIFF_V_W),
                            page_table, lw, lam_init)

        ym_p = _prompt_mla_attn(qm, kvbf, bp, t, lw)
        qm_s = qm[:, n_p:].reshape(MLA_HEADS, bs, tn, MLA_CACHE_W).transpose(1, 0, 2, 3)
        ym_s = _sample_attn('mla', l, qm_s.reshape(bs, MLA_HEADS * tn, MLA_CACHE_W), ckv_t, None,
                            kv[n_p:].reshape(bs, tn, MLA_CACHE_W), None, page_table, lw, lam_init)

        cat = lambda a, b: jnp.concatenate([a, b.reshape(n_s, -1)], axis=0)
        h, qx = _mix_out(x, cat(yrw_p, yrw_s), cat(yd_p, yd_s), cat(ym_p, ym_s), lw)

        mk, mv = _mem_kv(mem, lw)
        ox_p = _cross_attn(qx, 0, bp, t, mk.reshape(1, bp, mem_len, MEM_W), mv.reshape(1, bp, mem_len, MEM_W), 0)
        ox_s = _cross_attn(qx, n_p, bs, tn, cmk, cmv, l)
        ox = jnp.concatenate([ox_p, ox_s], axis=0)

        x = _ffn_dense(h, ox, lw) if l % 2 == 0 else _moe(h, ox, lw)

        outs['pS'].append(st_p)
        outs['psh'].append(xrw[:n_p].reshape(bp, t, RWKV_IN)[:, -1])
        outs['pdk'].append(ktok[:n_p].reshape(bp, t, 2, 2, DIFF_DK))
        outs['pdv'].append(vtok[:n_p].reshape(bp, t, 2, DIFF_DV))
        outs['pkv'].append(kv[:n_p].reshape(bp, t, MLA_CACHE_W))
        outs['pmk'].append(mk.reshape(bp, mem_len, MEM_HEADS, MEM_HEAD_DIM))
        outs['pmv'].append(mv.reshape(bp, mem_len, MEM_HEADS, MEM_HEAD_DIM))
        outs['sS'].append(st_s)
        outs['ssh'].append(xrw[n_p:].reshape(bs, tn, RWKV_IN)[:, -1])
        outs['sdk'].append(ktok[n_p:].reshape(bs, tn, 2, 2, DIFF_DK))
        outs['sdv'].append(vtok[n_p:].reshape(bs, tn, 2, DIFF_DV))
        outs['skv'].append(kv[n_p:].reshape(bs, tn, MLA_CACHE_W))

    g = final_norm.reshape(1, D_MODEL).astype(F32)
    y_prompt = _final_norm(x, 0, n_p, g).reshape(bp, t, D_MODEL)
    y_sample = _final_norm(x, n_p, n_s, g).reshape(bs, tn, D_MODEL)
    st = lambda k: jnp.stack(outs[k])
    return (y_prompt, y_sample, st('pS'), st('psh'), st('pdk'), st('pdv'), st('pkv'), st('pmk'), st('pmv'),
            st('sS'), st('ssh'), st('sdk'), st('sdv'), st('skv'))
```

```python
import functools
import math

import jax
import jax.numpy as jnp
from jax import lax
from jax.experimental import pallas as pl
from jax.experimental.pallas import tpu as pltpu

F32 = jnp.float32
BF16 = jnp.bfloat16

D_MODEL = 1024
PAGE_SIZE = 128
RWKV_HEADS = 4
RWKV_N = 64
RWKV_W = 256
DECAY_LORA = 64
ICLR_LORA = 64
GATE_LORA = 128
RWKV_IN = 1024
GN_EPS = 64e-5
DIFF_DK = 64
DIFF_DV = 128
DIFF_Q_W = 512
DIFF_K_W = 256
DIFF_V_W = 256
SUBLN_EPS = 1e-5
MLA_HEADS = 4
MLA_NOPE = 64
MLA_ROPE = 32
MLA_V = 64
MLA_Q_LORA = 192
MLA_KV_LORA = 128
MLA_CACHE_W = 160
ROPE_THETA = 10000.0
MEM_HEADS = 4
MEM_HEAD_DIM = 64
MEM_W = 256
N_EXPERTS = 8
RMS_EPS = 1e-6
NEG_BIG = -1e30
LANES = 128

VMEM_LIMIT_BYTES = 56 * 1024 * 1024
MOE_CHUNK = 128
PAGES_PER_STEP = 16
RWKV_CHUNK = 64


def _cparams(*sem):
    return pltpu.CompilerParams(dimension_semantics=sem, vmem_limit_bytes=VMEM_LIMIT_BYTES)


def _tile(n, pref):
    t = pref
    while t > 8 and n % t:
        t //= 2
    assert n % t == 0, (n, pref)
    return t


def _dot(a, b):
    return jnp.dot(a, b, preferred_element_type=F32)


def _dot_nt(a, b):
    return lax.dot_general(a, b, (((1,), (1,)), ((), ())), preferred_element_type=F32)


def _dot_tn(a, b):
    return lax.dot_general(a, b, (((0,), (0,)), ((), ())), preferred_element_type=F32)


def _dot_hi(a, b):
    return jnp.dot(a, b, preferred_element_type=F32, precision=lax.Precision.HIGHEST)


def _rms(x, g, eps):
    return x * lax.rsqrt(jnp.mean(x * x, axis=-1, keepdims=True) + eps) * g


def _rep(x, n):
    return x if n == 1 else jnp.concatenate([x] * n, axis=1)


def _full(shape):
    nd = len(shape)
    return pl.BlockSpec(shape, lambda *a, _nd=nd: (0,) * _nd)


def _mixer_in_kernel(x_ref, g_ref, wrw, wq, wk, wv, wcq, wckv, wkr, wkrr, cos_ref, sin_ref, gq_ref, gkv_ref,
                     wqn, wqr, wqrr, wukt,
                     xrw_o, qd_o, ktok_o, kh_o, vtok_o, vbf_o, kv_o, kvbf_o, qm_o, *, mla_scale):
    n = _rms(x_ref[...], g_ref[...], RMS_EPS).astype(BF16)
    xrw_o[...] = _dot(n, wrw[...])
    q = _dot(n, wq[...]).astype(BF16)
    for j in range(8):
        qd_o[j] = q[:, j * DIFF_DK:(j + 1) * DIFF_DK]
    k = _dot(n, wk[...])
    ktok_o[...] = k
    kb = k.astype(BF16)
    for j in range(4):
        kh_o[j] = kb[:, j * DIFF_DK:(j + 1) * DIFF_DK]
    v = _dot(n, wv[...])
    vtok_o[...] = v
    vbf_o[...] = v.astype(BF16)
    cos = cos_ref[...]
    sin = sin_ref[...]
    ckvn = _rms(_dot(n, wckv[...]), gkv_ref[...], RMS_EPS)
    kr = _dot(n, wkr[...]) * cos + _dot(n, wkrr[...]) * sin
    kv_o[:, :MLA_KV_LORA] = ckvn
    kv_o[:, MLA_KV_LORA:] = kr
    kvbf_o[:, :MLA_KV_LORA] = ckvn.astype(BF16)
    kvbf_o[:, MLA_KV_LORA:] = kr.astype(BF16)
    cqn = _rms(_dot(n, wcq[...]), gq_ref[...], RMS_EPS).astype(BF16)
    for h in range(MLA_HEADS):
        qn = _dot(cqn, wqn[h]).astype(BF16)
        qlat = _dot(qn, wukt[h])
        qr = _dot(cqn, wqr[h]) * cos + _dot(cqn, wqrr[h]) * sin
        qm_o[h, :, :MLA_KV_LORA] = (qlat * mla_scale).astype(BF16)
        qm_o[h, :, MLA_KV_LORA:] = (qr * mla_scale).astype(BF16)


def _mixer_in(x, lw, cos, sin):
    n = x.shape[0]
    tm = _tile(n, 512)
    row = lambda w: pl.BlockSpec((tm, w), lambda i: (i, 0))
    slab = lambda s, w: pl.BlockSpec((s, tm, w), lambda i: (0, i, 0))
    ws = [lw['w_rw'], lw['w_q'], lw['w_k'], lw['w_v'], lw['w_cq'], lw['w_ckv'], lw['w_kr'], lw['w_krr']]
    tail = [lw['g_q'], lw['g_kv'], lw['wq_nope'], lw['wq_rope'], lw['wq_rope_rot'], lw['wuk_t']]
    in_specs = ([row(D_MODEL), _full((1, D_MODEL))] + [_full(w.shape) for w in ws]
                + [row(MLA_ROPE), row(MLA_ROPE)] + [_full(w.shape) for w in tail])
    out_shape = [
        jax.ShapeDtypeStruct((n, RWKV_IN), F32),
        jax.ShapeDtypeStruct((8, n, DIFF_DK), BF16),
        jax.ShapeDtypeStruct((n, DIFF_K_W), F32),
        jax.ShapeDtypeStruct((4, n, DIFF_DK), BF16),
        jax.ShapeDtypeStruct((n, DIFF_V_W), F32),
        jax.ShapeDtypeStruct((n, DIFF_V_W), BF16),
        jax.ShapeDtypeStruct((n, MLA_CACHE_W), F32),
        jax.ShapeDtypeStruct((n, MLA_CACHE_W), BF16),
        jax.ShapeDtypeStruct((MLA_HEADS, n, MLA_CACHE_W), BF16),
    ]
    out_specs = [row(RWKV_IN), slab(8, DIFF_DK), row(DIFF_K_W), slab(4, DIFF_DK), row(DIFF_V_W), row(DIFF_V_W),
                 row(MLA_CACHE_W), row(MLA_CACHE_W), slab(MLA_HEADS, MLA_CACHE_W)]
    return pl.pallas_call(
        functools.partial(_mixer_in_kernel, mla_scale=(MLA_NOPE + MLA_ROPE) ** -0.5),
        grid=(n // tm,), in_specs=in_specs, out_specs=out_specs, out_shape=out_shape,
        compiler_params=_cparams("parallel"), name="mixer_in",
    )(x, lw['g_mix'], *ws, cos, sin, *tail)


def _rwkv_prep_kernel(x_ref, prev_ref, s0_ref, mu_ref, w0_ref, a0_ref, kkw_ref, ka_ref, rk_ref, w2_ref, a2_ref,
                      g2_ref, bones_ref, r_o, lw_o, kf_o, v_o, nkk_o, kka_o, g_o, bonus_o):
    i = pl.program_id(1)
    x = x_ref[...]
    first = jnp.where(i == 0, s0_ref[0], prev_ref[7:8, :])
    rowid = lax.broadcasted_iota(jnp.int32, x.shape, 0)
    shifted = jnp.where(rowid == 0, first, pltpu.roll(x, 1, 0))
    xm = x + (shifted - x) * mu_ref[...]
    r = xm[:, 0:RWKV_W]
    k = xm[:, RWKV_W:2 * RWKV_W]
    v = xm[:, 2 * RWKV_W:3 * RWKV_W]
    c = 3 * RWKV_W
    xw = xm[:, c:c + DECAY_LORA]
    xa = xm[:, c + DECAY_LORA:c + DECAY_LORA + ICLR_LORA]
    xg = xm[:, c + DECAY_LORA + ICLR_LORA:]
    z = -(w0_ref[...] + _dot(jnp.tanh(xw).astype(BF16), w2_ref[...]))
    softplus = jnp.maximum(z, 0.0) + jnp.log(1.0 + jnp.exp(-jnp.abs(z)))
    logw = -jnp.exp(-softplus - 0.5)
    a = jax.nn.sigmoid(a0_ref[...] + _dot(xa.astype(BF16), a2_ref[...]))
    g = _dot(jax.nn.sigmoid(xg).astype(BF16), g2_ref[...])
    kkr = k * kkw_ref[...]
    kk = kkr / jnp.maximum(jnp.sqrt(_dot_hi(kkr * kkr, bones_ref[...])), 1e-12)
    kf = k * (1.0 + (a - 1.0) * ka_ref[...])
    bonus = _dot_hi(r * kf * rk_ref[...], bones_ref[...]) * v
    g_o[...] = g
    bonus_o[...] = bonus
    nkk = -kk
    kka = kk * a
    for h in range(RWKV_HEADS):
        sl = slice(h * RWKV_N, (h + 1) * RWKV_N)
        r_o[0, h] = r[:, sl]
        lw_o[0, h] = logw[:, sl]
        kf_o[0, h] = kf[:, sl]
        v_o[0, h] = v[:, sl]
        nkk_o[0, h] = nkk[:, sl]
        kka_o[0, h] = kka[:, sl]


def _rwkv_prep(xrw, row0, nb, t, shift0, lw):
    tm = _tile(t, 512)
    nt = t // tm
    blk0 = row0 // tm
    blk0_8 = row0 // 8
    in_specs = [
        pl.BlockSpec((tm, RWKV_IN), lambda b, i: (blk0 + b * nt + i, 0)),
        pl.BlockSpec((8, RWKV_IN), lambda b, i: (jnp.maximum(blk0_8 + (b * t + i * tm) // 8 - 1, 0), 0)),
        pl.BlockSpec((1, 1, RWKV_IN), lambda b, i: (b, 0, 0)),
    ]
    params = [lw['rw_mu'], lw['rw_w0'], lw['rw_a0'], lw['rw_kk'], lw['rw_ka'], lw['rw_rk'], lw['rw_w2'],
              lw['rw_a2'], lw['rw_g2'], lw['bones']]
    in_specs += [_full(p.shape) for p in params]
    hm = pl.BlockSpec((1, RWKV_HEADS, tm, RWKV_N), lambda b, i: (b, 0, i, 0))
    tokm = pl.BlockSpec((tm, RWKV_W), lambda b, i: (b * nt + i, 0))
    hm_shape = jax.ShapeDtypeStruct((nb, RWKV_HEADS, t, RWKV_N), F32)
    tok_shape = jax.ShapeDtypeStruct((nb * t, RWKV_W), F32)
    return pl.pallas_call(
        _rwkv_prep_kernel, grid=(nb, nt), in_specs=in_specs,
        out_specs=[hm] * 6 + [tokm] * 2, out_shape=[hm_shape] * 6 + [tok_shape] * 2,
        compiler_params=_cparams("parallel", "arbitrary"), name="rwkv_prep",
    )(xrw, xrw, shift0.reshape(nb, 1, RWKV_IN), *params)


def _rwkv_epilogue(y, h, g_ref, bonus_ref, lnw_ref, lnb_ref, y_o):
    sl = slice(h * RWKV_N, (h + 1) * RWKV_N)
    mu = jnp.mean(y, axis=-1, keepdims=True)
    yc = y - mu
    var = jnp.mean(yc * yc, axis=-1, keepdims=True)
    yn = yc * lax.rsqrt(var + GN_EPS) * lnw_ref[:, sl] + lnb_ref[:, sl]
    y_o[:, sl] = (yn + bonus_ref[:, sl]) * g_ref[:, sl]


def _rwkv_scan_kernel(r_ref, lw_ref, kf_ref, v_ref, nkk_ref, kka_ref, s0_ref, g_ref, bonus_ref, lnw_ref, lnb_ref,
                      y_o, st_o, s_sc, y_sc, *, tb):
    i = pl.program_id(1)

    @pl.when(i == 0)
    def _():
        s_sc[...] = s0_ref[0]

    eye = (lax.broadcasted_iota(jnp.int32, (RWKV_N, RWKV_N), 0)
           == lax.broadcasted_iota(jnp.int32, (RWKV_N, RWKV_N), 1)).astype(F32)

    def step(t, states):
        out = []
        for h in range(RWKV_HEADS):
            s = states[h]
            row = lambda ref: ref[0, h, pl.ds(t, 1), :]
            sa = jnp.sum(s * row(nkk_ref), axis=-1, keepdims=True)
            v_col = jnp.sum(eye * row(v_ref), axis=-1, keepdims=True)
            s = s * jnp.exp(row(lw_ref)) + sa * row(kka_ref) + v_col * row(kf_ref)
            y_col = jnp.sum(s * row(r_ref), axis=-1, keepdims=True)
            y_sc[h, pl.ds(t, 1), :] = jnp.sum(eye * y_col, axis=0, keepdims=True)
            out.append(s)
        return tuple(out)

    states = lax.fori_loop(0, tb, step, tuple(s_sc[h] for h in range(RWKV_HEADS)))
    for h in range(RWKV_HEADS):
        s_sc[h] = states[h]
        _rwkv_epilogue(y_sc[h], h, g_ref, bonus_ref, lnw_ref, lnb_ref, y_o)

    @pl.when(i == pl.num_programs(1) - 1)
    def _():
        st_o[0] = s_sc[...]


def _rwkv_scan(prep, s0, nb, t, lw):
    r, logw, kf, v, nkk, kka, g, bonus = prep
    tb = _tile(t, 128)
    nt = t // tb
    hm = pl.BlockSpec((1, RWKV_HEADS, tb, RWKV_N), lambda b, i: (b, 0, i, 0))
    st = pl.BlockSpec((1, RWKV_HEADS, RWKV_N, RWKV_N), lambda b, i: (b, 0, 0, 0))
    tokm = pl.BlockSpec((tb, RWKV_W), lambda b, i: (b * nt + i, 0))
    return pl.pallas_call(
        functools.partial(_rwkv_scan_kernel, tb=tb), grid=(nb, nt),
        in_specs=[hm] * 6 + [st, tokm, tokm, _full((1, RWKV_W)), _full((1, RWKV_W))],
        out_specs=[tokm, st],
        out_shape=[jax.ShapeDtypeStruct((nb * t, RWKV_W), F32),
                   jax.ShapeDtypeStruct((nb, RWKV_HEADS, RWKV_N, RWKV_N), F32)],
        scratch_shapes=[pltpu.VMEM((RWKV_HEADS, RWKV_N, RWKV_N), F32), pltpu.VMEM((RWKV_HEADS, tb, RWKV_N), F32)],
        compiler_params=_cparams("parallel", "arbitrary"), name="rwkv_scan",
    )(r, logw, kf, v, nkk, kka, s0, g, bonus, lw['rw_lnw'], lw['rw_lnb'])


def _rwkv_chunk_prep_kernel(r_ref, lw_ref, kf_ref, nkk_ref, kka_ref,
                            abar_o, rbar_o, bhat_o, khat_o, gam_o, aab_o, aak_o, arb_o, ark_o):
    c = RWKV_CHUNK
    ti = lax.broadcasted_iota(jnp.int32, (c, c), 0)
    si = lax.broadcasted_iota(jnp.int32, (c, c), 1)
    strict = si < ti
    incl = si <= ti
    tril = incl.astype(F32)
    for h in range(RWKV_HEADS):
        logw = lw_ref[0, h]
        cum = _dot_hi(tril, logw)
        gam = jnp.exp(cum)
        ginv = jnp.exp(-cum)
        gc = gam[c - 1:c]
        abar = nkk_ref[0, h] * jnp.exp(cum - logw)
        rbar = r_ref[0, h] * gam
        bt = kka_ref[0, h] * ginv
        kt = kf_ref[0, h] * ginv
        abar_o[0, h] = abar
        rbar_o[0, h] = rbar
        bhat_o[0, h] = bt * gc
        khat_o[0, h] = kt * gc
        gam_o[0, h] = gam
        left = jnp.concatenate([abar, rbar], axis=0).astype(BF16)
        right = jnp.concatenate([bt, kt], axis=0).astype(BF16)
        p = _dot_nt(left, right)
        aab_o[0, h] = jnp.where(strict, p[:c, :c], 0.0)
        aak_o[0, h] = jnp.where(strict, p[:c, c:], 0.0)
        arb_o[0, h] = jnp.where(incl, p[c:, :c], 0.0)
        ark_o[0, h] = jnp.where(incl, p[c:, c:], 0.0)


def _tri_inv_kernel(a_ref, t_ref):
    c, _, nb = a_ref.shape
    jrow = lax.broadcasted_iota(jnp.int32, (c, nb), 0)

    def row(i, carry):
        def inner(m, acc):
            return acc + a_ref[i, pl.ds(m, 1), :] * t_ref[m]
        t_ref[i] = lax.fori_loop(0, i, inner, (jrow == i).astype(F32))
        return carry

    lax.fori_loop(0, c, row, 0)


def _rwkv_chunk_scan_kernel(abar_ref, rbar_ref, bhat_ref, khat_ref, v_ref, gam_ref, t_ref, aak_ref, arb_ref, ark_ref,
                            s0_ref, g_ref, bonus_ref, lnw_ref, lnb_ref, y_o, st_o, s_sc):
    i = pl.program_id(1)
    c = RWKV_CHUNK

    @pl.when(i == 0)
    def _():
        s_sc[...] = s0_ref[0]

    for h in range(RWKV_HEADS):
        s = s_sc[h]
        v = v_ref[0, h]
        left = jnp.concatenate([abar_ref[0, h], rbar_ref[0, h]], axis=0).astype(BF16)
        m1 = _dot_nt(left, s.astype(BF16))
        base = m1[:c] + _dot(aak_ref[0, h].astype(BF16), v.astype(BF16))
        u = _dot(t_ref[0, h].astype(BF16), base.astype(BF16))
        uv = jnp.concatenate([u, v], axis=0).astype(BF16)
        lower = jnp.concatenate([arb_ref[0, h], ark_ref[0, h]], axis=1).astype(BF16)
        y = m1[c:] + _dot(lower, uv)
        bk = jnp.concatenate([bhat_ref[0, h], khat_ref[0, h]], axis=0).astype(BF16)
        s_sc[h] = s * gam_ref[0, h][c - 1:c] + _dot_tn(uv, bk)
        _rwkv_epilogue(y, h, g_ref, bonus_ref, lnw_ref, lnb_ref, y_o)

    @pl.when(i == pl.num_programs(1) - 1)
    def _():
        st_o[0] = s_sc[...]


def _rwkv_chunked(prep, s0, nb, t, lw):
    r, logw, kf, v, nkk, kka, g, bonus = prep
    c = RWKV_CHUNK
    nc = t // c
    hm = pl.BlockSpec((1, RWKV_HEADS, c, RWKV_N), lambda b, i: (b, 0, i, 0))
    hm_shape = jax.ShapeDtypeStruct((nb, RWKV_HEADS, t, RWKV_N), F32)
    abar, rbar, bhat, khat, gam, aab, aak, arb, ark = pl.pallas_call(
        _rwkv_chunk_prep_kernel, grid=(nb, nc), in_specs=[hm] * 5, out_specs=[hm] * 9, out_shape=[hm_shape] * 9,
        compiler_params=_cparams("parallel", "parallel"), name="rwkv_chunk_prep",
    )(r, logw, kf, nkk, kka)

    nbatch = nb * RWKV_HEADS * nc
    bn = nbatch if nbatch < 2 * LANES else 2 * LANES
    assert nbatch % bn == 0
    a_t = aab.reshape(nb, RWKV_HEADS, nc, c, c).transpose(3, 4, 0, 1, 2).reshape(c, c, nbatch)
    blk = pl.BlockSpec((c, c, bn), lambda j: (0, 0, j))
    t_t = pl.pallas_call(
        _tri_inv_kernel, grid=(nbatch // bn,), in_specs=[blk], out_specs=blk,
        out_shape=jax.ShapeDtypeStruct((c, c, nbatch), F32),
        compiler_params=_cparams("parallel"), name="rwkv_tri_inv",
    )(a_t)
    tinv = t_t.reshape(c, c, nb, RWKV_HEADS, nc).transpose(2, 3, 4, 0, 1).reshape(nb, RWKV_HEADS, t, c)

    st = pl.BlockSpec((1, RWKV_HEADS, RWKV_N, RWKV_N), lambda b, i: (b, 0, 0, 0))
    tokm = pl.BlockSpec((c, RWKV_W), lambda b, i: (b * nc + i, 0))
    return pl.pallas_call(
        _rwkv_chunk_scan_kernel, grid=(nb, nc),
        in_specs=[hm] * 10 + [st, tokm, tokm, _full((1, RWKV_W)), _full((1, RWKV_W))],
        out_specs=[tokm, st],
        out_shape=[jax.ShapeDtypeStruct((nb * t, RWKV_W), F32),
                   jax.ShapeDtypeStruct((nb, RWKV_HEADS, RWKV_N, RWKV_N), F32)],
        scratch_shapes=[pltpu.VMEM((RWKV_HEADS, RWKV_N, RWKV_N), F32)],
        compiler_params=_cparams("parallel", "arbitrary"), name="rwkv_chunk_scan",
    )(abar, rbar, bhat, khat, v, gam, tinv, aak, arb, ark, s0, g, bonus, lw['rw_lnw'], lw['rw_lnb'])


def _rwkv_mix(prep, s0, nb, t, lw):
    if t % RWKV_CHUNK == 0:
        return _rwkv_chunked(prep, s0, nb, t, lw)
    return _rwkv_scan(prep, s0, nb, t, lw)


def _diff_lambda(lam_ref, lam_init):
    lp = lam_ref[...]
    s1 = jnp.sum(lp[0:1] * lp[1:2], axis=-1, keepdims=True)
    s2 = jnp.sum(lp[2:3] * lp[3:4], axis=-1, keepdims=True)
    return jnp.exp(s1) - jnp.exp(s2) + lam_init


def _flash_kernel(qi_tab, ki_tab, *refs, mode, tq, lam_init):
    if mode == 'diff':
        q_ref, k_ref, v_ref, lam_ref, g_ref, o_ref, m_sc, l_sc, acc_sc = refs
        pid = pl.program_id(2)
        n_maps = 2
    else:
        q_ref, kv_ref, wuv_ref, o_ref, m_sc, l_sc, acc_sc = refs
        pid = pl.program_id(1)
        n_maps = 1
    qi = qi_tab[pid]
    ki = ki_tab[pid]
    rows = m_sc.shape[1]

    @pl.when(ki == 0)
    def _():
        m_sc[...] = jnp.full(m_sc.shape, NEG_BIG, F32)
        l_sc[...] = jnp.zeros(l_sc.shape, F32)
        acc_sc[...] = jnp.zeros(acc_sc.shape, F32)

    def update(masked):
        for m in range(n_maps):
            if mode == 'diff':
                q = q_ref[2 * m:2 * m + 2].reshape(rows, DIFF_DK)
                k = k_ref[m]
                v = v_ref[...]
            else:
                q = q_ref[...].reshape(rows, MLA_CACHE_W)
                k = kv_ref[...]
                v = k[:, :MLA_KV_LORA]
            s = _dot_nt(q, k)
            if masked:
                qpos = lax.broadcasted_iota(jnp.int32, s.shape, 0) % tq
                kpos = lax.broadcasted_iota(jnp.int32, s.shape, 1)
                s = jnp.where(kpos <= qpos, s, NEG_BIG)
            m_prev = m_sc[m]
            m_new = jnp.maximum(m_prev, jnp.max(s, axis=-1, keepdims=True))
            alpha = jnp.exp(m_prev - m_new)
            p = jnp.exp(s - _rep(m_new, s.shape[1] // LANES))
            l_sc[m] = alpha * l_sc[m] + jnp.sum(p, axis=-1, keepdims=True)
            acc_sc[m] = alpha * acc_sc[m] + _dot(p.astype(BF16), v)
            m_sc[m] = m_new

    @pl.when(ki < qi)
    def _():
        update(False)

    @pl.when(ki == qi)
    def _():
        update(True)
        if mode == 'diff':
            lam = _diff_lambda(lam_ref, lam_init)
            o = acc_sc[0] / l_sc[0] - lam * (acc_sc[1] / l_sc[1])
            y = _rms(o, g_ref[...], SUBLN_EPS) * (1.0 - lam_init)
            for r in range(2):
                o_ref[:, r * DIFF_DV:(r + 1) * DIFF_DV] = y[r * tq:(r + 1) * tq]
        else:
            o = (acc_sc[0] / l_sc[0]).astype(BF16)
            for h in range(MLA_HEADS):
                o_ref[:, h * MLA_V:(h + 1) * MLA_V] = _dot(o[h * tq:(h + 1) * tq], wuv_ref[h])


def _pair_tables(nq):
    qi = [q for q in range(nq) for _ in range(q + 1)]
    ki = [k for q in range(nq) for k in range(q + 1)]
    return jnp.asarray(qi, jnp.int32), jnp.asarray(ki, jnp.int32)


def _prompt_diff_attn(qd, kh, vbf, nb, t, lw, lam_init):
    tq = _tile(t, 512)
    nq = t // tq
    qi_tab, ki_tab = _pair_tables(nq)
    rows = 2 * tq
    grid_spec = pltpu.PrefetchScalarGridSpec(
        num_scalar_prefetch=2, grid=(nb, 2, len(qi_tab)),
        in_specs=[
            pl.BlockSpec((4, tq, DIFF_DK), lambda b, h, p, qt, kt: (h, b * nq + qt[p], 0)),
            pl.BlockSpec((2, tq, DIFF_DK), lambda b, h, p, qt, kt: (h, b * nq + kt[p], 0)),
            pl.BlockSpec((tq, DIFF_DV), lambda b, h, p, qt, kt: (b * nq + kt[p], h)),
            pl.BlockSpec((4, DIFF_DK), lambda b, h, p, qt, kt: (0, 0)),
            pl.BlockSpec((1, DIFF_DV), lambda b, h, p, qt, kt: (0, 0)),
        ],
        out_specs=pl.BlockSpec((tq, 2 * DIFF_DV), lambda b, h, p, qt, kt: (b * nq + qt[p], h)),
        scratch_shapes=[pltpu.VMEM((2, rows, LANES), F32), pltpu.VMEM((2, rows, LANES), F32),
                        pltpu.VMEM((2, rows, DIFF_DV), F32)],
    )
    return pl.pallas_call(
        functools.partial(_flash_kernel, mode='diff', tq=tq, lam_init=lam_init), grid_spec=grid_spec,
        out_shape=jax.ShapeDtypeStruct((nb * t, 4 * DIFF_DV), F32),
        compiler_params=_cparams("parallel", "parallel", "arbitrary"), name="prompt_diff_attn",
    )(qi_tab, ki_tab, qd, kh, vbf, lw['diff_lambda'], lw['diff_subln'])


def _prompt_mla_attn(qm, kvbf, nb, t, lw):
    tq = _tile(t, 256)
    nq = t // tq
    qi_tab, ki_tab = _pair_tables(nq)
    rows = MLA_HEADS * tq
    grid_spec = pltpu.PrefetchScalarGridSpec(
        num_scalar_prefetch=2, grid=(nb, len(qi_tab)),
        in_specs=[
            pl.BlockSpec((MLA_HEADS, tq, MLA_CACHE_W), lambda b, p, qt, kt: (0, b * nq + qt[p], 0)),
            pl.BlockSpec((tq, MLA_CACHE_W), lambda b, p, qt, kt: (b * nq + kt[p], 0)),
            pl.BlockSpec((MLA_HEADS, MLA_KV_LORA, MLA_V), lambda b, p, qt, kt: (0, 0, 0)),
        ],
        out_specs=pl.BlockSpec((tq, MLA_HEADS * MLA_V), lambda b, p, qt, kt: (b * nq + qt[p], 0)),
        scratch_shapes=[pltpu.VMEM((1, rows, LANES), F32), pltpu.VMEM((1, rows, LANES), F32),
                        pltpu.VMEM((1, rows, MLA_KV_LORA), F32)],
    )
    return pl.pallas_call(
        functools.partial(_flash_kernel, mode='mla', tq=tq, lam_init=0.0), grid_spec=grid_spec,
        out_shape=jax.ShapeDtypeStruct((nb * t, MLA_HEADS * MLA_V), F32),
        compiler_params=_cparams("parallel", "arbitrary"), name="prompt_mla_attn",
    )(qi_tab, ki_tab, qm, kvbf, lw['wuv'])


def _decode_kernel(pt_ref, *refs, mode, layer, npg, tn, lam_init):
    if mode == 'diff':
        (q_ref, ck_hbm, cv_hbm, kn_ref, vn_ref, lam_ref, g_ref, o_ref,
         kbuf, vbuf, sem, m_sc, l_sc, acc_sc) = refs
        groups = 4
    else:
        q_ref, ck_hbm, kn_ref, wuv_ref, o_ref, kbuf, sem, m_sc, l_sc, acc_sc = refs
        groups = 1
    b = pl.program_id(0)
    c = pl.program_id(1)
    nchunk = pl.num_programs(1)
    step = b * nchunk + c
    slot = step % 2
    rows = q_ref.shape[1]
    gr = rows // groups

    def page_copies(seq, chunk, sl):
        out = []
        for j in range(npg):
            page = pt_ref[seq, chunk * npg + j]
            out.append(pltpu.make_async_copy(ck_hbm.at[layer, page], kbuf.at[sl, j], sem.at[0, sl, j]))
            if mode == 'diff':
                out.append(pltpu.make_async_copy(cv_hbm.at[layer, page], vbuf.at[sl, j], sem.at[1, sl, j]))
        return out

    @pl.when(step == 0)
    def _():
        for cp in page_copies(b, c, slot):
            cp.start()

    nxt = step + 1

    @pl.when(nxt < pl.num_programs(0) * nchunk)
    def _():
        for cp in page_copies(nxt // nchunk, nxt % nchunk, 1 - slot):
            cp.start()

    @pl.when(c == 0)
    def _():
        m_sc[...] = jnp.full(m_sc.shape, NEG_BIG, F32)
        l_sc[...] = jnp.zeros(l_sc.shape, F32)
        acc_sc[...] = jnp.zeros(acc_sc.shape, F32)

    for cp in page_copies(b, c, slot):
        cp.wait()

    q = q_ref[0]
    if mode == 'diff':
        parts = []
        for h in range(2):
            for m in range(2):
                kcat = jnp.concatenate([kbuf[slot, j, h, m].astype(BF16) for j in range(npg)], axis=1)
                g = h * 2 + m
                parts.append(_dot(q[g * gr:(g + 1) * gr], kcat))
        s = jnp.concatenate(parts, axis=0)
    else:
        kcat = jnp.concatenate([kbuf[slot, j].astype(BF16) for j in range(npg)], axis=1)
        s = _dot(q, kcat)
    m_prev = m_sc[...]
    m_new = jnp.maximum(m_prev, jnp.max(s, axis=-1, keepdims=True))
    alpha = jnp.exp(m_prev - m_new)
    p = jnp.exp(s - m_new).astype(BF16)
    l_sc[...] = alpha * l_sc[...] + jnp.sum(p.astype(F32), axis=-1, keepdims=True)
    if mode == 'diff':
        nrow = 2 * PAGE_SIZE
        ri = lax.broadcasted_iota(jnp.int32, (nrow, nrow), 0)
        ci = lax.broadcasted_iota(jnp.int32, (nrow, nrow), 1)
        sel = (ci == 2 * (ri % PAGE_SIZE) + ri // PAGE_SIZE).astype(BF16)
        split = [_dot(sel, vbuf[slot, j].astype(BF16)).astype(BF16) for j in range(npg)]
        pv = jnp.concatenate(
            [_dot(p[h * 2 * gr:(h + 1) * 2 * gr],
                  jnp.concatenate([sp[h * PAGE_SIZE:(h + 1) * PAGE_SIZE] for sp in split], axis=0))
             for h in range(2)], axis=0)
    else:
        pv = _dot_nt(p, kcat[:MLA_KV_LORA])
    acc_sc[...] = alpha * acc_sc[...] + pv
    m_sc[...] = m_new

    @pl.when(c == pl.num_programs(1) - 1)
    def _():
        qf = q.astype(F32)
        kn = kn_ref[0].astype(BF16).astype(F32)
        vn = (vn_ref[0] if mode == 'diff' else kn_ref[0][:, :MLA_KV_LORA]).astype(BF16).astype(F32)
        tok = lax.broadcasted_iota(jnp.int32, (rows, 1), 0) % tn
        sn = []
        for j in range(tn):
            if mode == 'diff':
                dots = [jnp.sum(qf[g * gr:(g + 1) * gr] * kn[j:j + 1, g * DIFF_DK:(g + 1) * DIFF_DK],
                                axis=-1, keepdims=True) for g in range(groups)]
                sj = jnp.concatenate(dots, axis=0)
            else:
                sj = jnp.sum(qf * kn[j:j + 1], axis=-1, keepdims=True)
            sn.append(jnp.where(tok >= j, sj, NEG_BIG))
        m_prev = m_sc[...]
        m_new = m_prev
        for sj in sn:
            m_new = jnp.maximum(m_new, sj)
        alpha = jnp.exp(m_prev - m_new)
        l = alpha * l_sc[...]
        acc = alpha * acc_sc[...]
        for j, sj in enumerate(sn):
            pj = jnp.exp(sj - m_new)
            l = l + pj
            if mode == 'diff':
                vrow = jnp.concatenate(
                    [jnp.broadcast_to(vn[j:j + 1, h * DIFF_DV:(h + 1) * DIFF_DV], (2 * gr, DIFF_DV)) for h in range(2)],
                    axis=0)
            else:
                vrow = vn[j:j + 1]
            acc = acc + pj * vrow
        o = acc / l
        if mode == 'diff':
            lam = _diff_lambda(lam_ref, lam_init)
            for h in range(2):
                base = h * 2 * gr
                oh = o[base:base + gr] - lam * o[base + gr:base + 2 * gr]
                y = _rms(oh, g_ref[...], SUBLN_EPS) * (1.0 - lam_init)
                for r in range(2):
                    col = (h * 2 + r) * DIFF_DV
                    o_ref[0, :, col:col + DIFF_DV] = y[r * tn:(r + 1) * tn]
        else:
            ob = o.astype(BF16)
            for h in range(MLA_HEADS):
                o_ref[0, :, h * MLA_V:(h + 1) * MLA_V] = _dot(ob[h * tn:(h + 1) * tn], wuv_ref[h])


def _sample_attn(mode, layer, q, cache_k, cache_v, knew, vnew, page_table, lw, lam_init):
    nb, rows, dk = q.shape
    n_pages = page_table.shape[1]
    npg = min(PAGES_PER_STEP, n_pages)
    assert n_pages % npg == 0
    tn = knew.shape[1]

    seq3 = lambda b, c, pt: (b, 0, 0)
    hbm = pl.BlockSpec(memory_space=pl.ANY)
    page_buf = lambda cache: pltpu.VMEM((2, npg) + cache.shape[2:], F32)
    in_specs = [pl.BlockSpec((1, rows, dk), seq3), hbm]
    operands = [q, cache_k]
    scratch = [page_buf(cache_k)]
    if mode == 'diff':
        in_specs += [hbm, pl.BlockSpec((1, tn, knew.shape[-1]), seq3), pl.BlockSpec((1, tn, vnew.shape[-1]), seq3),
                     pl.BlockSpec((4, DIFF_DK), lambda b, c, pt: (0, 0)),
                     pl.BlockSpec((1, DIFF_DV), lambda b, c, pt: (0, 0))]
        operands += [cache_v, knew, vnew, lw['diff_lambda'], lw['diff_subln']]
        scratch += [page_buf(cache_v)]
        dv, width = DIFF_DV, 4 * DIFF_DV
    else:
        in_specs += [pl.BlockSpec((1, tn, knew.shape[-1]), seq3),
                     pl.BlockSpec((MLA_HEADS, MLA_KV_LORA, MLA_V), lambda b, c, pt: (0, 0, 0))]
        operands += [knew, lw['wuv']]
        dv, width = MLA_KV_LORA, MLA_HEADS * MLA_V
    scratch += [pltpu.SemaphoreType.DMA((2, 2, npg)),
                pltpu.VMEM((rows, 1), F32), pltpu.VMEM((rows, 1), F32), pltpu.VMEM((rows, dv), F32)]
    grid_spec = pltpu.PrefetchScalarGridSpec(
        num_scalar_prefetch=1, grid=(nb, n_pages // npg), in_specs=in_specs,
        out_specs=pl.BlockSpec((1, tn, width), seq3), scratch_shapes=scratch,
    )
    return pl.pallas_call(
        functools.partial(_decode_kernel, mode=mode, layer=layer, npg=npg, tn=tn, lam_init=lam_init),
        grid_spec=grid_spec, out_shape=jax.ShapeDtypeStruct((nb, tn, width), F32),
        compiler_params=_cparams("arbitrary", "arbitrary"), name="sample_%s_attn" % mode,
    )(page_table, *operands)


def _mix_out_kernel(x_ref, yrw_ref, yd_ref, ym_ref, w1, w2, w3, gx_ref, wxq, h_o, qx_o):
    h = (x_ref[...] + _dot(yrw_ref[...].astype(BF16), w1[...]) + _dot(yd_ref[...].astype(BF16), w2[...])
         + _dot(ym_ref[...].astype(BF16), w3[...]))
    h_o[...] = h
    qx_o[...] = _dot(_rms(h, gx_ref[...], RMS_EPS).astype(BF16), wxq[...])


def _mix_out(x, yrw, yd, ym, lw):
    n = x.shape[0]
    tm = _tile(n, 512)
    row = lambda w: pl.BlockSpec((tm, w), lambda i: (i, 0))
    ws = [lw['w_out_rw'], lw['w_out_d'], lw['w_out_m'], lw['g_cross'], lw['w_xq']]
    return pl.pallas_call(
        _mix_out_kernel, grid=(n // tm,),
        in_specs=[row(D_MODEL), row(RWKV_W), row(4 * DIFF_DV), row(MLA_HEADS * MLA_V)] + [_full(w.shape) for w in ws],
        out_specs=[row(D_MODEL), row(MEM_W)],
        out_shape=[jax.ShapeDtypeStruct((n, D_MODEL), F32), jax.ShapeDtypeStruct((n, MEM_W), F32)],
        compiler_params=_cparams("parallel"), name="mix_out",
    )(x, yrw, yd, ym, *ws)


def _mem_kv_kernel(x_ref, g_ref, wk, wv, k_o, v_o):
    n = _rms(x_ref[...], g_ref[...], RMS_EPS).astype(BF16)
    k_o[...] = _dot(n, wk[...])
    v_o[...] = _dot(n, wv[...])


def _mem_kv(mem, lw):
    n = mem.shape[0]
    tm = _tile(n, 512)
    row = lambda w: pl.BlockSpec((tm, w), lambda i: (i, 0))
    return pl.pallas_call(
        _mem_kv_kernel, grid=(n // tm,),
        in_specs=[row(D_MODEL), _full((1, D_MODEL)), _full(lw['w_xk'].shape), _full(lw['w_xv'].shape)],
        out_specs=[row(MEM_W), row(MEM_W)],
        out_shape=[jax.ShapeDtypeStruct((n, MEM_W), F32)] * 2,
        compiler_params=_cparams("parallel"), name="mem_kv",
    )(mem, lw['g_mem'], lw['w_xk'], lw['w_xv'])


def _cross_kernel(q_ref, k_ref, v_ref, o_ref):
    q = q_ref[...].astype(BF16)
    k = k_ref[0, 0].astype(BF16)
    v = v_ref[0, 0].astype(BF16)
    for h in range(MEM_HEADS):
        sl = slice(h * MEM_HEAD_DIM, (h + 1) * MEM_HEAD_DIM)
        s = _dot_nt(q[:, sl], k[:, sl])
        p = jnp.exp(s - jnp.max(s, axis=-1, keepdims=True))
        p = p / jnp.sum(p, axis=-1, keepdims=True)
        o_ref[:, sl] = _dot(p.astype(BF16), v[:, sl])


def _cross_attn(qx, row0, nseq, t, mk, mv, layer):
    tq = _tile(t, 512)
    nt = t // tq
    blk0 = row0 // tq
    mem = pl.BlockSpec((1, 1) + mk.shape[2:], lambda s, i: (layer, s, 0, 0))
    return pl.pallas_call(
        _cross_kernel, grid=(nseq, nt),
        in_specs=[pl.BlockSpec((tq, MEM_W), lambda s, i: (blk0 + s * nt + i, 0)), mem, mem],
        out_specs=pl.BlockSpec((tq, MEM_W), lambda s, i: (s * nt + i, 0)),
        out_shape=jax.ShapeDtypeStruct((nseq * t, MEM_W), F32),
        compiler_params=_cparams("parallel", "arbitrary"), name="cross_attn",
    )(qx, mk, mv)


def _ffn_kernel(h_ref, ox_ref, wxo, g_ref, wg, wu, wd, o_ref, h2_sc, n_sc, acc_sc):
    f = pl.program_id(1)

    @pl.when(f == 0)
    def _():
        h2 = h_ref[...] + _dot(ox_ref[...].astype(BF16), wxo[...])
        h2_sc[...] = h2
        n_sc[...] = _rms(h2, g_ref[...], RMS_EPS).astype(BF16)
        acc_sc[...] = jnp.zeros(acc_sc.shape, F32)

    n = n_sc[...]
    hg = _dot(n, wg[...])
    acc_sc[...] += _dot((jax.nn.silu(hg) * _dot(n, wu[...])).astype(BF16), wd[...])

    @pl.when(f == pl.num_programs(1) - 1)
    def _():
        o_ref[...] = h2_sc[...] + acc_sc[...]


def _ffn_dense(h, ox, lw):
    n = h.shape[0]
    tm = _tile(n, 1024)
    dff = lw['ffn_wg'].shape[1]
    tf = 256
    assert dff % tf == 0
    row = lambda w: pl.BlockSpec((tm, w), lambda i, f: (i, 0))
    return pl.pallas_call(
        _ffn_kernel, grid=(n // tm, dff // tf),
        in_specs=[row(D_MODEL), row(MEM_W), pl.BlockSpec((MEM_W, D_MODEL), lambda i, f: (0, 0)),
                  pl.BlockSpec((1, D_MODEL), lambda i, f: (0, 0)),
                  pl.BlockSpec((D_MODEL, tf), lambda i, f: (0, f)), pl.BlockSpec((D_MODEL, tf), lambda i, f: (0, f)),
                  pl.BlockSpec((tf, D_MODEL), lambda i, f: (f, 0))],
        out_specs=row(D_MODEL), out_shape=jax.ShapeDtypeStruct((n, D_MODEL), F32),
        scratch_shapes=[pltpu.VMEM((tm, D_MODEL), F32), pltpu.VMEM((tm, D_MODEL), BF16),
                        pltpu.VMEM((tm, D_MODEL), F32)],
        compiler_params=_cparams("parallel", "arbitrary"), name="ffn_dense",
    )(h, ox, lw['w_xo'], lw['g_ffn'], lw['ffn_wg'], lw['ffn_wu'], lw['ffn_wd'])


def _router_kernel(h_ref, ox_ref, wxo, g_ref, wr, h2_o, n_o, comb_o, pos_o, cnt_o):
    h2 = h_ref[...] + _dot(ox_ref[...].astype(BF16), wxo[...])
    h2_o[...] = h2
    nf = _rms(h2, g_ref[...], RMS_EPS)
    n_o[...] = nf.astype(BF16)
    logits = _dot_hi(nf, wr[...])
    tm = logits.shape[0]
    eid = lax.broadcasted_iota(jnp.int32, logits.shape, 1)
    m1 = jnp.max(logits, axis=-1, keepdims=True)
    i1 = jnp.min(jnp.where(logits == m1, eid, N_EXPERTS), axis=-1, keepdims=True)
    sel1 = eid == i1
    rest = jnp.where(sel1, -jnp.inf, logits)
    m2 = jnp.max(rest, axis=-1, keepdims=True)
    i2 = jnp.min(jnp.where(rest == m2, eid, N_EXPERTS), axis=-1, keepdims=True)
    sel2 = eid == i2
    e2 = jnp.exp(m2 - m1)
    den = 1.0 + e2
    comb_o[...] = jnp.where(sel1, 1.0 / den, 0.0) + jnp.where(sel2, e2 / den, 0.0)
    sel = jnp.logical_or(sel1, sel2).astype(F32)
    lower = (lax.broadcasted_iota(jnp.int32, (tm, tm), 1) < lax.broadcasted_iota(jnp.int32, (tm, tm), 0))
    rank = _dot(lower.astype(BF16), sel.astype(BF16))
    pos_o[...] = jnp.where(sel > 0.0, rank, -1.0)
    cnt_o[0] = jnp.sum(sel, axis=0, keepdims=True)


def _router(h, ox, lw, tm):
    n = h.shape[0]
    row = lambda w: pl.BlockSpec((tm, w), lambda i: (i, 0))
    return pl.pallas_call(
        _router_kernel, grid=(n // tm,),
        in_specs=[row(D_MODEL), row(MEM_W), _full((MEM_W, D_MODEL)), _full((1, D_MODEL)),
                  _full((D_MODEL, N_EXPERTS))],
        out_specs=[row(D_MODEL), row(D_MODEL), row(N_EXPERTS), row(N_EXPERTS),
                   pl.BlockSpec((1, 1, N_EXPERTS), lambda i: (i, 0, 0))],
        out_shape=[jax.ShapeDtypeStruct((n, D_MODEL), F32), jax.ShapeDtypeStruct((n, D_MODEL), BF16),
                   jax.ShapeDtypeStruct((n, N_EXPERTS), F32), jax.ShapeDtypeStruct((n, N_EXPERTS), F32),
                   jax.ShapeDtypeStruct((n // tm, 1, N_EXPERTS), F32)],
        compiler_params=_cparams("parallel"), name="moe_router",
    )(h, ox, lw['w_xo'], lw['g_ffn'], lw['moe_router'])


def _moe_kernel(cnt_ref, n_ref, post_ref, pos_ref, comb_ref, h2_ref, wg, wu, wd, o_ref, xe_sc, acc_sc):
    i = pl.program_id(0)
    e = pl.program_id(1)
    f = pl.program_id(2)
    tm = n_ref.shape[0]
    ch = MOE_CHUNK
    nch = (cnt_ref[i * N_EXPERTS + e] + ch - 1) // ch

    @pl.when(jnp.logical_and(e == 0, f == 0))
    def _():
        o_ref[...] = h2_ref[...]

    @pl.when(f == 0)
    def _():
        esub = lax.broadcasted_iota(jnp.int32, (N_EXPERTS, tm), 0)
        pos_row = jnp.sum(jnp.where(esub == e, post_ref[...], 0.0), axis=0, keepdims=True)
        slot = lax.broadcasted_iota(jnp.int32, (ch, tm), 0).astype(F32)

        def gather(c, carry):
            base = pl.multiple_of(c * ch, ch)
            onehot = (pos_row == slot + (c * ch).astype(F32)).astype(BF16)
            xe_sc[pl.ds(base, ch), :] = _dot(onehot, n_ref[...]).astype(BF16)
            acc_sc[pl.ds(base, ch), :] = jnp.zeros((ch, D_MODEL), F32)
            return carry

        lax.fori_loop(0, nch, gather, 0)

    def expert(c, carry):
        base = pl.multiple_of(c * ch, ch)
        x = xe_sc[pl.ds(base, ch), :]
        hh = (jax.nn.silu(_dot(x, wg[0])) * _dot(x, wu[0])).astype(BF16)
        acc_sc[pl.ds(base, ch), :] += _dot(hh, wd[0])
        return carry

    lax.fori_loop(0, nch, expert, 0)

    @pl.when(f == pl.num_programs(2) - 1)
    def _():
        elane = lax.broadcasted_iota(jnp.int32, (tm, N_EXPERTS), 1)
        pos_col = jnp.sum(jnp.where(elane == e, pos_ref[...], 0.0), axis=-1, keepdims=True)
        gate_col = jnp.sum(jnp.where(elane == e, comb_ref[...], 0.0), axis=-1, keepdims=True)
        slot = lax.broadcasted_iota(jnp.int32, (tm, ch), 1).astype(F32)

        def scatter(c, carry):
            base = pl.multiple_of(c * ch, ch)
            onehot = (pos_col == slot + (c * ch).astype(F32)).astype(BF16)
            y = acc_sc[pl.ds(base, ch), :].astype(BF16)
            o_ref[...] += gate_col * _dot(onehot, y)
            return carry

        lax.fori_loop(0, nch, scatter, 0)


def _moe(h, ox, lw):
    n = h.shape[0]
    tm = _tile(n, 1024)
    h2, nb, comb, pos, cnt = _router(h, ox, lw, tm)
    pos_t = pos.T
    cnt = cnt.reshape(-1).astype(jnp.int32)
    dff = lw['moe_wg'].shape[2]
    tf = 896 if dff % 896 == 0 else 512
    assert dff % tf == 0
    tok = lambda w: pl.BlockSpec((tm, w), lambda i, e, f, c: (i, 0))
    grid_spec = pltpu.PrefetchScalarGridSpec(
        num_scalar_prefetch=1, grid=(n // tm, N_EXPERTS, dff // tf),
        in_specs=[tok(D_MODEL), pl.BlockSpec((N_EXPERTS, tm), lambda i, e, f, c: (0, i)),
                  tok(N_EXPERTS), tok(N_EXPERTS), tok(D_MODEL),
                  pl.BlockSpec((1, D_MODEL, tf), lambda i, e, f, c: (e, 0, f)),
                  pl.BlockSpec((1, D_MODEL, tf), lambda i, e, f, c: (e, 0, f)),
                  pl.BlockSpec((1, tf, D_MODEL), lambda i, e, f, c: (e, f, 0))],
        out_specs=tok(D_MODEL),
        scratch_shapes=[pltpu.VMEM((tm, D_MODEL), BF16), pltpu.VMEM((tm, D_MODEL), F32)],
    )
    return pl.pallas_call(
        _moe_kernel, grid_spec=grid_spec, out_shape=jax.ShapeDtypeStruct((n, D_MODEL), F32),
        compiler_params=_cparams("parallel", "arbitrary", "arbitrary"), name="moe_experts",
    )(cnt, nb, pos_t, pos, comb, h2, lw['moe_wg'], lw['moe_wu'], lw['moe_wd'])


def _final_norm_kernel(x_ref, g_ref, o_ref):
    o_ref[...] = _rms(x_ref[...], g_ref[...], RMS_EPS)


def _final_norm(x, row0, rows, g):
    tm = _tile(rows, 1024)
    blk0 = row0 // tm
    return pl.pallas_call(
        _final_norm_kernel, grid=(rows // tm,),
        in_specs=[pl.BlockSpec((tm, D_MODEL), lambda i: (blk0 + i, 0)), _full((1, D_MODEL))],
        out_specs=pl.BlockSpec((tm, D_MODEL), lambda i: (i, 0)),
        out_shape=jax.ShapeDtypeStruct((rows, D_MODEL), F32),
        compiler_params=_cparams("parallel"), name="final_norm",
    )(x, g)


def _rot_cols(w):
    half = w.shape[-1] // 2
    return jnp.concatenate([-w[..., half:], w[..., :half]], axis=-1)


def _layer_weights(p, l):
    bf = lambda a: a.astype(BF16)
    r2 = lambda a: a.reshape(1, -1).astype(F32)
    w_in = p['w_in'][l]
    c1 = RWKV_IN
    c2 = c1 + DIFF_Q_W
    c3 = c2 + DIFF_K_W
    c4 = c3 + DIFF_V_W
    c5 = c4 + MLA_Q_LORA
    c6 = c5 + MLA_KV_LORA
    wq = w_in[:, c1:c2].reshape(D_MODEL, 2, 2, 2, DIFF_DK).transpose(0, 1, 3, 2, 4).reshape(D_MODEL, DIFF_Q_W)
    wuq = p['mla_wuq'][l]
    w_out = p['w_out'][l]
    lw = dict(
        g_mix=r2(p['norm_mix'][l]), w_rw=bf(w_in[:, :c1]), w_q=bf(wq * DIFF_DK ** -0.5), w_k=bf(w_in[:, c2:c3]),
        w_v=bf(w_in[:, c3:c4]), w_cq=bf(w_in[:, c4:c5]), w_ckv=bf(w_in[:, c5:c6]), w_kr=bf(w_in[:, c6:]),
        w_krr=bf(_rot_cols(w_in[:, c6:])), g_q=r2(p['mla_qnorm'][l]), g_kv=r2(p['mla_kvnorm'][l]),
        wq_nope=bf(wuq[:, :, :MLA_NOPE].transpose(1, 0, 2)), wq_rope=bf(wuq[:, :, MLA_NOPE:].transpose(1, 0, 2)),
        wq_rope_rot=bf(_rot_cols(wuq[:, :, MLA_NOPE:]).transpose(1, 0, 2)),
        wuk_t=bf(p['mla_wuk'][l].transpose(1, 2, 0)), wuv=bf(p['mla_wuv'][l].transpose(1, 0, 2)),
        rw_mu=r2(p['rwkv_mu'][l]), rw_w0=r2(p['rwkv_w0'][l]), rw_a0=r2(p['rwkv_a0'][l]), rw_kk=r2(p['rwkv_kk'][l]),
        rw_ka=r2(p['rwkv_ka'][l]), rw_rk=r2(p['rwkv_rk'][l]), rw_w2=bf(p['rwkv_w2'][l]), rw_a2=bf(p['rwkv_a2'][l]),
        rw_g2=bf(p['rwkv_g2'][l]), rw_lnw=r2(p['rwkv_lnw'][l]), rw_lnb=r2(p['rwkv_lnb'][l]),
        bones=jnp.kron(jnp.eye(RWKV_HEADS, dtype=F32), jnp.ones((RWKV_N, RWKV_N), F32)),
        diff_lambda=p['diff_lambda'][l].astype(F32), diff_subln=r2(p['diff_subln'][l]),
        w_out_rw=bf(w_out[:RWKV_W]), w_out_d=bf(w_out[RWKV_W:RWKV_W + 4 * DIFF_DV]),
        w_out_m=bf(w_out[RWKV_W + 4 * DIFF_DV:]), g_cross=r2(p['norm_cross'][l]),
        w_xq=bf(p['w_xq'][l] * MEM_HEAD_DIM ** -0.5), g_mem=r2(p['norm_mem'][l]), w_xk=bf(p['w_xk'][l]),
        w_xv=bf(p['w_xv'][l]), w_xo=bf(p['w_xo'][l]), g_ffn=r2(p['norm_ffn'][l]),
    )
    i = l // 2
    if l % 2 == 0:
        lw.update(ffn_wg=bf(p['ffn_wg'][i]), ffn_wu=bf(p['ffn_wu'][i]), ffn_wd=bf(p['ffn_wd'][i]))
    else:
        lw.update(moe_router=p['moe_router'][i].astype(F32), moe_wg=bf(p['moe_wg'][i]), moe_wu=bf(p['moe_wu'][i]),
                  moe_wd=bf(p['moe_wd'][i]))
    return lw


def _rope_tables(pos):
    half = MLA_ROPE // 2
    inv = ROPE_THETA ** (-jnp.arange(half, dtype=F32) / half)
    ang = pos.astype(F32)[:, None] * inv[None, :]
    cos, sin = jnp.cos(ang), jnp.sin(ang)
    return jnp.concatenate([cos, cos], axis=-1), jnp.concatenate([sin, sin], axis=-1)


def kernel(x_prompt, x_sample, state_rwkv, state_shift, cache_diff_k, cache_diff_v, cache_mla_kv, cache_mem_k, cache_mem_v, page_table, mem_prompt, norm_mix, w_in, rwkv_mu, rwkv_w0, rwkv_w2, rwkv_a0, rwkv_a2, rwkv_g2, rwkv_kk, rwkv_ka, rwkv_rk, rwkv_lnw, rwkv_lnb, diff_lambda, diff_subln, mla_qnorm, mla_kvnorm, mla_wuq, mla_wuk, mla_wuv, w_out, norm_cross, norm_mem, w_xq, w_xk, w_xv, w_xo, norm_ffn, ffn_wg, ffn_wu, ffn_wd, moe_router, moe_wg, moe_wu, moe_wd, final_norm):
    p = dict(norm_mix=norm_mix, w_in=w_in, rwkv_mu=rwkv_mu, rwkv_w0=rwkv_w0, rwkv_w2=rwkv_w2, rwkv_a0=rwkv_a0,
             rwkv_a2=rwkv_a2, rwkv_g2=rwkv_g2, rwkv_kk=rwkv_kk, rwkv_ka=rwkv_ka, rwkv_rk=rwkv_rk,
             rwkv_lnw=rwkv_lnw, rwkv_lnb=rwkv_lnb, diff_lambda=diff_lambda, diff_subln=diff_subln,
             mla_qnorm=mla_qnorm, mla_kvnorm=mla_kvnorm, mla_wuq=mla_wuq, mla_wuk=mla_wuk, mla_wuv=mla_wuv,
             w_out=w_out, norm_cross=norm_cross, norm_mem=norm_mem, w_xq=w_xq, w_xk=w_xk, w_xv=w_xv, w_xo=w_xo,
             norm_ffn=norm_ffn, ffn_wg=ffn_wg, ffn_wu=ffn_wu, ffn_wd=ffn_wd, moe_router=moe_router, moe_wg=moe_wg,
             moe_wu=moe_wu, moe_wd=moe_wd)
    bp, t = x_prompt.shape[:2]
    bs, tn = x_sample.shape[:2]
    depth = w_in.shape[0]
    n_p, n_s = bp * t, bs * tn
    past = page_table.shape[1] * PAGE_SIZE
    mem_len = mem_prompt.shape[1]

    x = jnp.concatenate([x_prompt.reshape(n_p, D_MODEL), x_sample.reshape(n_s, D_MODEL)], axis=0)
    pos = jnp.concatenate([jnp.tile(jnp.arange(t), bp), jnp.tile(past + jnp.arange(tn), bs)])
    cos, sin = _rope_tables(pos)
    ck_t = cache_diff_k.transpose(0, 1, 3, 4, 5, 2)
    ckv_t = cache_mla_kv.transpose(0, 1, 3, 2)
    cv_rows = cache_diff_v.reshape(depth, cache_diff_v.shape[1], 2 * PAGE_SIZE, DIFF_DV)
    cmk = cache_mem_k.reshape(depth, bs, mem_len, MEM_W)
    cmv = cache_mem_v.reshape(depth, bs, mem_len, MEM_W)
    mem = mem_prompt.reshape(bp * mem_len, D_MODEL)
    zeros_shift = jnp.zeros((bp, RWKV_IN), F32)
    zeros_state = jnp.zeros((bp, RWKV_HEADS, RWKV_N, RWKV_N), F32)

    outs = {k: [] for k in ('pS', 'psh', 'pdk', 'pdv', 'pkv', 'pmk', 'pmv', 'sS', 'ssh', 'sdk', 'sdv', 'skv')}
    for l in range(depth):
        lw = _layer_weights(p, l)
        lam_init = 0.8 - 0.6 * math.exp(-0.3 * l)
        xrw, qd, ktok, kh, vtok, vbf, kv, kvbf, qm = _mixer_in(x, lw, cos, sin)

        yrw_p, st_p = _rwkv_mix(_rwkv_prep(xrw, 0, bp, t, zeros_shift, lw), zeros_state, bp, t, lw)
        yrw_s, st_s = _rwkv_mix(_rwkv_prep(xrw, n_p, bs, tn, state_shift[l], lw), state_rwkv[l], bs, tn, lw)

        yd_p = _prompt_diff_attn(qd, kh, vbf, bp, t, lw, lam_init)
        qd_s = qd[:, n_p:].reshape(8, bs, tn, DIFF_DK).transpose(1, 0, 2, 3).reshape(bs, 8 * tn, DIFF_DK)
        yd_s = _sample_attn('diff', l, qd_s, ck_t, cv_rows,
                            ktok[n_p:].reshape(bs, tn, DIFF_K_W), vtok[n_p:].reshape(bs, tn, DIFF_V_W),
                            page_table, lw, lam_init)

        ym_p = _prompt_mla_attn(qm, kvbf, bp, t, lw)
        qm_s = qm[:, n_p:].reshape(MLA_HEADS, bs, tn, MLA_CACHE_W).transpose(1, 0, 2, 3)
        ym_s = _sample_attn('mla', l, qm_s.reshape(bs, MLA_HEADS * tn, MLA_CACHE_W), ckv_t, None,
                            kv[n_p:].reshape(bs, tn, MLA_CACHE_W), None, page_table, lw, lam_init)

        cat = lambda a, b: jnp.concatenate([a, b.reshape(n_s, -1)], axis=0)
        h, qx = _mix_out(x, cat(yrw_p, yrw_s), cat(yd_p, yd_s), cat(ym_p, ym_s), lw)

        mk, mv = _mem_kv(mem, lw)
        ox_p = _cross_attn(qx, 0, bp, t, mk.reshape(1, bp, mem_len, MEM_W), mv.reshape(1, bp, mem_len, MEM_W), 0)
        ox_s = _cross_attn(qx, n_p, bs, tn, cmk, cmv, l)
        ox = jnp.concatenate([ox_p, ox_s], axis=0)

        x = _ffn_dense(h, ox, lw) if l % 2 == 0 else _moe(h, ox, lw)

        outs['pS'].append(st_p)
        outs['psh'].append(xrw[:n_p].reshape(bp, t, RWKV_IN)[:, -1])
        outs['pdk'].append(ktok[:n_p].reshape(bp, t, 2, 2, DIFF_DK))
        outs['pdv'].append(vtok[:n_p].reshape(bp, t, 2, DIFF_DV))
        outs['pkv'].append(kv[:n_p].reshape(bp, t, MLA_CACHE_W))
        outs['pmk'].append(mk.reshape(bp, mem_len, MEM_HEADS, MEM_HEAD_DIM))
        outs['pmv'].append(mv.reshape(bp, mem_len, MEM_HEADS, MEM_HEAD_DIM))
        outs['sS'].append(st_s)
        outs['ssh'].append(xrw[n_p:].reshape(bs, tn, RWKV_IN)[:, -1])
        outs['sdk'].append(ktok[n_p:].reshape(bs, tn, 2, 2, DIFF_DK))
        outs['sdv'].append(vtok[n_p:].reshape(bs, tn, 2, DIFF_DV))
        outs['skv'].append(kv[n_p:].reshape(bs, tn, MLA_CACHE_W))

    g = final_norm.reshape(1, D_MODEL).astype(F32)
    y_prompt = _final_norm(x, 0, n_p, g).reshape(bp, t, D_MODEL)
    y_sample = _final_norm(x, n_p, n_s, g).reshape(bs, tn, D_MODEL)
    st = lambda k: jnp.stack(outs[k])
    return (y_prompt, y_sample, st('pS'), st('psh'), st('pdk'), st('pdv'), st('pkv'), st('pmk'), st('pmv'),
            st('sS'), st('ssh'), st('sdk'), st('sdv'), st('skv'))
```

```python
import functools
import math

import jax
import jax.numpy as jnp
from jax import lax
from jax.experimental import pallas as pl
from jax.experimental.pallas import tpu as pltpu

F32 = jnp.float32
BF16 = jnp.bfloat16

D_MODEL = 1024
PAGE_SIZE = 128
RWKV_HEADS = 4
RWKV_N = 64
RWKV_W = 256
DECAY_LORA = 64
ICLR_LORA = 64
GATE_LORA = 128
RWKV_IN = 1024
GN_EPS = 64e-5
DIFF_DK = 64
DIFF_DV = 128
DIFF_Q_W = 512
DIFF_K_W = 256
DIFF_V_W = 256
SUBLN_EPS = 1e-5
MLA_HEADS = 4
MLA_NOPE = 64
MLA_ROPE = 32
MLA_V = 64
MLA_Q_LORA = 192
MLA_KV_LORA = 128
MLA_CACHE_W = 160
ROPE_THETA = 10000.0
MEM_HEADS = 4
MEM_HEAD_DIM = 64
MEM_W = 256
N_EXPERTS = 8
RMS_EPS = 1e-6
NEG_BIG = -1e30
LANES = 128

VMEM_LIMIT_BYTES = 56 * 1024 * 1024
MOE_CHUNK = 128
PAGES_PER_STEP = 16
RWKV_CHUNK = 64


def _cparams(*sem):
    return pltpu.CompilerParams(dimension_semantics=sem, vmem_limit_bytes=VMEM_LIMIT_BYTES)


def _tile(n, pref):
    t = pref
    while t > 8 and n % t:
        t //= 2
    assert n % t == 0, (n, pref)
    return t


def _dot(a, b):
    return jnp.dot(a, b, preferred_element_type=F32)


def _dot_nt(a, b):
    return lax.dot_general(a, b, (((1,), (1,)), ((), ())), preferred_element_type=F32)


def _dot_tn(a, b):
    return lax.dot_general(a, b, (((0,), (0,)), ((), ())), preferred_element_type=F32)


def _dot_hi(a, b):
    return jnp.dot(a, b, preferred_element_type=F32, precision=lax.Precision.HIGHEST)


def _rms(x, g, eps):
    return x * lax.rsqrt(jnp.mean(x * x, axis=-1, keepdims=True) + eps) * g


def _rep(x, n):
    return x if n == 1 else jnp.concatenate([x] * n, axis=1)


def _full(shape):
    nd = len(shape)
    return pl.BlockSpec(shape, lambda *a, _nd=nd: (0,) * _nd)


def _mixer_in_kernel(x_ref, g_ref, wrw, wq, wk, wv, wcq, wckv, wkr, wkrr, cos_ref, sin_ref, gq_ref, gkv_ref,
                     wqn, wqr, wqrr, wukt,
                     xrw_o, qd_o, ktok_o, kh_o, vtok_o, vbf_o, kv_o, kvbf_o, qm_o, *, mla_scale):
    n = _rms(x_ref[...], g_ref[...], RMS_EPS).astype(BF16)
    xrw_o[...] = _dot(n, wrw[...])
    q = _dot(n, wq[...]).astype(BF16)
    for j in range(8):
        qd_o[j] = q[:, j * DIFF_DK:(j + 1) * DIFF_DK]
    k = _dot(n, wk[...])
    ktok_o[...] = k
    kb = k.astype(BF16)
    for j in range(4):
        kh_o[j] = kb[:, j * DIFF_DK:(j + 1) * DIFF_DK]
    v = _dot(n, wv[...])
    vtok_o[...] = v
    vbf_o[...] = v.astype(BF16)
    cos = cos_ref[...]
    sin = sin_ref[...]
    ckvn = _rms(_dot(n, wckv[...]), gkv_ref[...], RMS_EPS)
    kr = _dot(n, wkr[...]) * cos + _dot(n, wkrr[...]) * sin
    kv_o[:, :MLA_KV_LORA] = ckvn
    kv_o[:, MLA_KV_LORA:] = kr
    kvbf_o[:, :MLA_KV_LORA] = ckvn.astype(BF16)
    kvbf_o[:, MLA_KV_LORA:] = kr.astype(BF16)
    cqn = _rms(_dot(n, wcq[...]), gq_ref[...], RMS_EPS).astype(BF16)
    for h in range(MLA_HEADS):
        qn = _dot(cqn, wqn[h]).astype(BF16)
        qlat = _dot(qn, wukt[h])
        qr = _dot(cqn, wqr[h]) * cos + _dot(cqn, wqrr[h]) * sin
        qm_o[h, :, :MLA_KV_LORA] = (qlat * mla_scale).astype(BF16)
        qm_o[h, :, MLA_KV_LORA:] = (qr * mla_scale).astype(BF16)


def _mixer_in(x, lw, cos, sin):
    n = x.shape[0]
    tm = _tile(n, 512)
    row = lambda w: pl.BlockSpec((tm, w), lambda i: (i, 0))
    slab = lambda s, w: pl.BlockSpec((s, tm, w), lambda i: (0, i, 0))
    ws = [lw['w_rw'], lw['w_q'], lw['w_k'], lw['w_v'], lw['w_cq'], lw['w_ckv'], lw['w_kr'], lw['w_krr']]
    tail = [lw['g_q'], lw['g_kv'], lw['wq_nope'], lw['wq_rope'], lw['wq_rope_rot'], lw['wuk_t']]
    in_specs = ([row(D_MODEL), _full((1, D_MODEL))] + [_full(w.shape) for w in ws]
                + [row(MLA_ROPE), row(MLA_ROPE)] + [_full(w.shape) for w in tail])
    out_shape = [
        jax.ShapeDtypeStruct((n, RWKV_IN), F32),
        jax.ShapeDtypeStruct((8, n, DIFF_DK), BF16),
        jax.ShapeDtypeStruct((n, DIFF_K_W), F32),
        jax.ShapeDtypeStruct((4, n, DIFF_DK), BF16),
        jax.ShapeDtypeStruct((n, DIFF_V_W), F32),
        jax.ShapeDtypeStruct((n, DIFF_V_W), BF16),
        jax.ShapeDtypeStruct((n, MLA_CACHE_W), F32),
        jax.ShapeDtypeStruct((n, MLA_CACHE_W), BF16),
        jax.ShapeDtypeStruct((MLA_HEADS, n, MLA_CACHE_W), BF16),
    ]
    out_specs = [row(RWKV_IN), slab(8, DIFF_DK), row(DIFF_K_W), slab(4, DIFF_DK), row(DIFF_V_W), row(DIFF_V_W),
                 row(MLA_CACHE_W), row(MLA_CACHE_W), slab(MLA_HEADS, MLA_CACHE_W)]
    return pl.pallas_call(
        functools.partial(_mixer_in_kernel, mla_scale=(MLA_NOPE + MLA_ROPE) ** -0.5),
        grid=(n // tm,), in_specs=in_specs, out_specs=out_specs, out_shape=out_shape,
        compiler_params=_cparams("parallel"), name="mixer_in",
    )(x, lw['g_mix'], *ws, cos, sin, *tail)


def _rwkv_prep_kernel(x_ref, prev_ref, s0_ref, mu_ref, w0_ref, a0_ref, kkw_ref, ka_ref, rk_ref, w2_ref, a2_ref,
                      g2_ref, bones_ref, r_o, lw_o, kf_o, v_o, nkk_o, kka_o, g_o, bonus_o):
    i = pl.program_id(1)
    x = x_ref[...]
    first = jnp.where(i == 0, s0_ref[0], prev_ref[7:8, :])
    rowid = lax.broadcasted_iota(jnp.int32, x.shape, 0)
    shifted = jnp.where(rowid == 0, first, pltpu.roll(x, 1, 0))
    xm = x + (shifted - x) * mu_ref[...]
    r = xm[:, 0:RWKV_W]
    k = xm[:, RWKV_W:2 * RWKV_W]
    v = xm[:, 2 * RWKV_W:3 * RWKV_W]
    c = 3 * RWKV_W
    xw = xm[:, c:c + DECAY_LORA]
    xa = xm[:, c + DECAY_LORA:c + DECAY_LORA + ICLR_LORA]
    xg = xm[:, c + DECAY_LORA + ICLR_LORA:]
    z = -(w0_ref[...] + _dot(jnp.tanh(xw).astype(BF16), w2_ref[...]))
    softplus = jnp.maximum(z, 0.0) + jnp.log(1.0 + jnp.exp(-jnp.abs(z)))
    logw = -jnp.exp(-softplus - 0.5)
    a = jax.nn.sigmoid(a0_ref[...] + _dot(xa.astype(BF16), a2_ref[...]))
    g = _dot(jax.nn.sigmoid(xg).astype(BF16), g2_ref[...])
    kkr = k * kkw_ref[...]
    kk = kkr / jnp.maximum(jnp.sqrt(_dot_hi(kkr * kkr, bones_ref[...])), 1e-12)
    kf = k * (1.0 + (a - 1.0) * ka_ref[...])
    bonus = _dot_hi(r * kf * rk_ref[...], bones_ref[...]) * v
    g_o[...] = g
    bonus_o[...] = bonus
    nkk = -kk
    kka = kk * a
    for h in range(RWKV_HEADS):
        sl = slice(h * RWKV_N, (h + 1) * RWKV_N)
        r_o[0, h] = r[:, sl]
        lw_o[0, h] = logw[:, sl]
        kf_o[0, h] = kf[:, sl]
        v_o[0, h] = v[:, sl]
        nkk_o[0, h] = nkk[:, sl]
        kka_o[0, h] = kka[:, sl]


def _rwkv_prep(xrw, row0, nb, t, shift0, lw):
    tm = _tile(t, 512)
    nt = t // tm
    blk0 = row0 // tm
    blk0_8 = row0 // 8
    in_specs = [
        pl.BlockSpec((tm, RWKV_IN), lambda b, i: (blk0 + b * nt + i, 0)),
        pl.BlockSpec((8, RWKV_IN), lambda b, i: (jnp.maximum(blk0_8 + (b * t + i * tm) // 8 - 1, 0), 0)),
        pl.BlockSpec((1, 1, RWKV_IN), lambda b, i: (b, 0, 0)),
    ]
    params = [lw['rw_mu'], lw['rw_w0'], lw['rw_a0'], lw['rw_kk'], lw['rw_ka'], lw['rw_rk'], lw['rw_w2'],
              lw['rw_a2'], lw['rw_g2'], lw['bones']]
    in_specs += [_full(p.shape) for p in params]
    hm = pl.BlockSpec((1, RWKV_HEADS, tm, RWKV_N), lambda b, i: (b, 0, i, 0))
    tokm = pl.BlockSpec((tm, RWKV_W), lambda b, i: (b * nt + i, 0))
    hm_shape = jax.ShapeDtypeStruct((nb, RWKV_HEADS, t, RWKV_N), F32)
    tok_shape = jax.ShapeDtypeStruct((nb * t, RWKV_W), F32)
    return pl.pallas_call(
        _rwkv_prep_kernel, grid=(nb, nt), in_specs=in_specs,
        out_specs=[hm] * 6 + [tokm] * 2, out_shape=[hm_shape] * 6 + [tok_shape] * 2,
        compiler_params=_cparams("parallel", "arbitrary"), name="rwkv_prep",
    )(xrw, xrw, shift0.reshape(nb, 1, RWKV_IN), *params)


def _rwkv_epilogue(y, h, g_ref, bonus_ref, lnw_ref, lnb_ref, y_o):
    sl = slice(h * RWKV_N, (h + 1) * RWKV_N)
    mu = jnp.mean(y, axis=-1, keepdims=True)
    yc = y - mu
    var = jnp.mean(yc * yc, axis=-1, keepdims=True)
    yn = yc * lax.rsqrt(var + GN_EPS) * lnw_ref[:, sl] + lnb_ref[:, sl]
    y_o[:, sl] = (yn + bonus_ref[:, sl]) * g_ref[:, sl]


def _rwkv_scan_kernel(r_ref, lw_ref, kf_ref, v_ref, nkk_ref, kka_ref, s0_ref, g_ref, bonus_ref, lnw_ref, lnb_ref,
                      y_o, st_o, s_sc, y_sc, *, tb):
    i = pl.program_id(1)

    @pl.when(i == 0)
    def _():
        s_sc[...] = s0_ref[0]

    eye = (lax.broadcasted_iota(jnp.int32, (RWKV_N, RWKV_N), 0)
           == lax.broadcasted_iota(jnp.int32, (RWKV_N, RWKV_N), 1)).astype(F32)

    def step(t, states):
        out = []
        for h in range(RWKV_HEADS):
            s = states[h]
            row = lambda ref: ref[0, h, pl.ds(t, 1), :]
            sa = jnp.sum(s * row(nkk_ref), axis=-1, keepdims=True)
            v_col = jnp.sum(eye * row(v_ref), axis=-1, keepdims=True)
            s = s * jnp.exp(row(lw_ref)) + sa * row(kka_ref) + v_col * row(kf_ref)
            y_col = jnp.sum(s * row(r_ref), axis=-1, keepdims=True)
            y_sc[h, pl.ds(t, 1), :] = jnp.sum(eye * y_col, axis=0, keepdims=True)
            out.append(s)
        return tuple(out)

    states = lax.fori_loop(0, tb, step, tuple(s_sc[h] for h in range(RWKV_HEADS)))
    for h in range(RWKV_HEADS):
        s_sc[h] = states[h]
        _rwkv_epilogue(y_sc[h], h, g_ref, bonus_ref, lnw_ref, lnb_ref, y_o)

    @pl.when(i == pl.num_programs(1) - 1)
    def _():
        st_o[0] = s_sc[...]


def _rwkv_scan(prep, s0, nb, t, lw):
    r, logw, kf, v, nkk, kka, g, bonus = prep
    tb = _tile(t, 128)
    nt = t // tb
    hm = pl.BlockSpec((1, RWKV_HEADS, tb, RWKV_N), lambda b, i: (b, 0, i, 0))
    st = pl.BlockSpec((1, RWKV_HEADS, RWKV_N, RWKV_N), lambda b, i: (b, 0, 0, 0))
    tokm = pl.BlockSpec((tb, RWKV_W), lambda b, i: (b * nt + i, 0))
    return pl.pallas_call(
        functools.partial(_rwkv_scan_kernel, tb=tb), grid=(nb, nt),
        in_specs=[hm] * 6 + [st, tokm, tokm, _full((1, RWKV_W)), _full((1, RWKV_W))],
        out_specs=[tokm, st],
        out_shape=[jax.ShapeDtypeStruct((nb * t, RWKV_W), F32),
                   jax.ShapeDtypeStruct((nb, RWKV_HEADS, RWKV_N, RWKV_N), F32)],
        scratch_shapes=[pltpu.VMEM((RWKV_HEADS, RWKV_N, RWKV_N), F32), pltpu.VMEM((RWKV_HEADS, tb, RWKV_N), F32)],
        compiler_params=_cparams("parallel", "arbitrary"), name="rwkv_scan",
    )(r, logw, kf, v, nkk, kka, s0, g, bonus, lw['rw_lnw'], lw['rw_lnb'])


def _rwkv_chunk_prep_kernel(r_ref, lw_ref, kf_ref, nkk_ref, kka_ref,
                            abar_o, rbar_o, bhat_o, khat_o, gam_o, aab_o, aak_o, arb_o, ark_o):
    c = RWKV_CHUNK
    ti = lax.broadcasted_iota(jnp.int32, (c, c), 0)
    si = lax.broadcasted_iota(jnp.int32, (c, c), 1)
    strict = si < ti
    incl = si <= ti
    tril = incl.astype(F32)
    for h in range(RWKV_HEADS):
        logw = lw_ref[0, h]
        cum = _dot_hi(tril, logw)
        gam = jnp.exp(cum)
        ginv = jnp.exp(-cum)
        gc = gam[c - 1:c]
        abar = nkk_ref[0, h] * jnp.exp(cum - logw)
        rbar = r_ref[0, h] * gam
        bt = kka_ref[0, h] * ginv
        kt = kf_ref[0, h] * ginv
        abar_o[0, h] = abar
        rbar_o[0, h] = rbar
        bhat_o[0, h] = bt * gc
        khat_o[0, h] = kt * gc
        gam_o[0, h] = gam
        left = jnp.concatenate([abar, rbar], axis=0).astype(BF16)
        right = jnp.concatenate([bt, kt], axis=0).astype(BF16)
        p = _dot_nt(left, right)
        aab_o[0, h] = jnp.where(strict, p[:c, :c], 0.0)
        aak_o[0, h] = jnp.where(strict, p[:c, c:], 0.0)
        arb_o[0, h] = jnp.where(incl, p[c:, :c], 0.0)
        ark_o[0, h] = jnp.where(incl, p[c:, c:], 0.0)


def _tri_inv_kernel(a_ref, t_ref):
    c, _, nb = a_ref.shape
    jrow = lax.broadcasted_iota(jnp.int32, (c, nb), 0)

    def row(i, carry):
        def inner(m, acc):
            return acc + a_ref[i, pl.ds(m, 1), :] * t_ref[m]
        t_ref[i] = lax.fori_loop(0, i, inner, (jrow == i).astype(F32))
        return carry

    lax.fori_loop(0, c, row, 0)


def _rwkv_chunk_scan_kernel(abar_ref, rbar_ref, bhat_ref, khat_ref, v_ref, gam_ref, t_ref, aak_ref, arb_ref, ark_ref,
                            s0_ref, g_ref, bonus_ref, lnw_ref, lnb_ref, y_o, st_o, s_sc):
    i = pl.program_id(1)
    c = RWKV_CHUNK

    @pl.when(i == 0)
    def _():
        s_sc[...] = s0_ref[0]

    for h in range(RWKV_HEADS):
        s = s_sc[h]
        v = v_ref[0, h]
        left = jnp.concatenate([abar_ref[0, h], rbar_ref[0, h]], axis=0).astype(BF16)
        m1 = _dot_nt(left, s.astype(BF16))
        base = m1[:c] + _dot(aak_ref[0, h].astype(BF16), v.astype(BF16))
        u = _dot(t_ref[0, h].astype(BF16), base.astype(BF16))
        uv = jnp.concatenate([u, v], axis=0).astype(BF16)
        lower = jnp.concatenate([arb_ref[0, h], ark_ref[0, h]], axis=1).astype(BF16)
        y = m1[c:] + _dot(lower, uv)
        bk = jnp.concatenate([bhat_ref[0, h], khat_ref[0, h]], axis=0).astype(BF16)
        s_sc[h] = s * gam_ref[0, h][c - 1:c] + _dot_tn(uv, bk)
        _rwkv_epilogue(y, h, g_ref, bonus_ref, lnw_ref, lnb_ref, y_o)

    @pl.when(i == pl.num_programs(1) - 1)
    def _():
        st_o[0] = s_sc[...]


def _rwkv_chunked(prep, s0, nb, t, lw):
    r, logw, kf, v, nkk, kka, g, bonus = prep
    c = RWKV_CHUNK
    nc = t // c
    hm = pl.BlockSpec((1, RWKV_HEADS, c, RWKV_N), lambda b, i: (b, 0, i, 0))
    hm_shape = jax.ShapeDtypeStruct((nb, RWKV_HEADS, t, RWKV_N), F32)
    abar, rbar, bhat, khat, gam, aab, aak, arb, ark = pl.pallas_call(
        _rwkv_chunk_prep_kernel, grid=(nb, nc), in_specs=[hm] * 5, out_specs=[hm] * 9, out_shape=[hm_shape] * 9,
        compiler_params=_cparams("parallel", "parallel"), name="rwkv_chunk_prep",
    )(r, logw, kf, nkk, kka)

    nbatch = nb * RWKV_HEADS * nc
    bn = nbatch if nbatch < 2 * LANES else 2 * LANES
    assert nbatch % bn == 0
    a_t = aab.reshape(nb, RWKV_HEADS, nc, c, c).transpose(3, 4, 0, 1, 2).reshape(c, c, nbatch)
    blk = pl.BlockSpec((c, c, bn), lambda j: (0, 0, j))
    t_t = pl.pallas_call(
        _tri_inv_kernel, grid=(nbatch // bn,), in_specs=[blk], out_specs=blk,
        out_shape=jax.ShapeDtypeStruct((c, c, nbatch), F32),
        compiler_params=_cparams("parallel"), name="rwkv_tri_inv",
    )(a_t)
    tinv = t_t.reshape(c, c, nb, RWKV_HEADS, nc).transpose(2, 3, 4, 0, 1).reshape(nb, RWKV_HEADS, t, c)

    st = pl.BlockSpec((1, RWKV_HEADS, RWKV_N, RWKV_N), lambda b, i: (b, 0, 0, 0))
    tokm = pl.BlockSpec((c, RWKV_W), lambda b, i: (b * nc + i, 0))
    return pl.pallas_call(
        _rwkv_chunk_scan_kernel, grid=(nb, nc),
        in_specs=[hm] * 10 + [st, tokm, tokm, _full((1, RWKV_W)), _full((1, RWKV_W))],
        out_specs=[tokm, st],
        out_shape=[jax.ShapeDtypeStruct((nb * t, RWKV_W), F32),
                   jax.ShapeDtypeStruct((nb, RWKV_HEADS, RWKV_N, RWKV_N), F32)],
        scratch_shapes=[pltpu.VMEM((RWKV_HEADS, RWKV_N, RWKV_N), F32)],
        compiler_params=_cparams("parallel", "arbitrary"), name="rwkv_chunk_scan",
    )(abar, rbar, bhat, khat, v, gam, tinv, aak, arb, ark, s0, g, bonus, lw['rw_lnw'], lw['rw_lnb'])


def _rwkv_mix(prep, s0, nb, t, lw):
    if t % RWKV_CHUNK == 0:
        return _rwkv_chunked(prep, s0, nb, t, lw)
    return _rwkv_scan(prep, s0, nb, t, lw)


def _diff_lambda(lam_ref, lam_init):
    lp = lam_ref[...]
    s1 = jnp.sum(lp[0:1] * lp[1:2], axis=-1, keepdims=True)
    s2 = jnp.sum(lp[2:3] * lp[3:4], axis=-1, keepdims=True)
    return jnp.exp(s1) - jnp.exp(s2) + lam_init


def _flash_kernel(qi_tab, ki_tab, *refs, mode, tq, lam_init):
    if mode == 'diff':
        q_ref, k_ref, v_ref, lam_ref, g_ref, o_ref, m_sc, l_sc, acc_sc = refs
        pid = pl.program_id(2)
        n_maps = 2
    else:
        q_ref, kv_ref, wuv_ref, o_ref, m_sc, l_sc, acc_sc = refs
        pid = pl.program_id(1)
        n_maps = 1
    qi = qi_tab[pid]
    ki = ki_tab[pid]
    rows = m_sc.shape[1]

    @pl.when(ki == 0)
    def _():
        m_sc[...] = jnp.full(m_sc.shape, NEG_BIG, F32)
        l_sc[...] = jnp.zeros(l_sc.shape, F32)
        acc_sc[...] = jnp.zeros(acc_sc.shape, F32)

    def update(masked):
        for m in range(n_maps):
            if mode == 'diff':
                q = q_ref[2 * m:2 * m + 2].reshape(rows, DIFF_DK)
                k = k_ref[m]
                v = v_ref[...]
            else:
                q = q_ref[...].reshape(rows, MLA_CACHE_W)
                k = kv_ref[...]
                v = k[:, :MLA_KV_LORA]
            s = _dot_nt(q, k)
            if masked:
                qpos = lax.broadcasted_iota(jnp.int32, s.shape, 0) % tq
                kpos = lax.broadcasted_iota(jnp.int32, s.shape, 1)
                s = jnp.where(kpos <= qpos, s, NEG_BIG)
            m_prev = m_sc[m]
            m_new = jnp.maximum(m_prev, jnp.max(s, axis=-1, keepdims=True))
            alpha = jnp.exp(m_prev - m_new)
            p = jnp.exp(s - _rep(m_new, s.shape[1] // LANES))
            l_sc[m] = alpha * l_sc[m] + jnp.sum(p, axis=-1, keepdims=True)
            acc_sc[m] = alpha * acc_sc[m] + _dot(p.astype(BF16), v)
            m_sc[m] = m_new

    @pl.when(ki < qi)
    def _():
        update(False)

    @pl.when(ki == qi)
    def _():
        update(True)
        if mode == 'diff':
            lam = _diff_lambda(lam_ref, lam_init)
            o = acc_sc[0] / l_sc[0] - lam * (acc_sc[1] / l_sc[1])
            y = _rms(o, g_ref[...], SUBLN_EPS) * (1.0 - lam_init)
            for r in range(2):
                o_ref[:, r * DIFF_DV:(r + 1) * DIFF_DV] = y[r * tq:(r + 1) * tq]
        else:
            o = (acc_sc[0] / l_sc[0]).astype(BF16)
            for h in range(MLA_HEADS):
                o_ref[:, h * MLA_V:(h + 1) * MLA_V] = _dot(o[h * tq:(h + 1) * tq], wuv_ref[h])


def _pair_tables(nq):
    qi = [q for q in range(nq) for _ in range(q + 1)]
    ki = [k for q in range(nq) for k in range(q + 1)]
    return jnp.asarray(qi, jnp.int32), jnp.asarray(ki, jnp.int32)


def _prompt_diff_attn(qd, kh, vbf, nb, t, lw, lam_init):
    tq = _tile(t, 512)
    nq = t // tq
    qi_tab, ki_tab = _pair_tables(nq)
    rows = 2 * tq
    grid_spec = pltpu.PrefetchScalarGridSpec(
        num_scalar_prefetch=2, grid=(nb, 2, len(qi_tab)),
        in_specs=[
            pl.BlockSpec((4, tq, DIFF_DK), lambda b, h, p, qt, kt: (h, b * nq + qt[p], 0)),
            pl.BlockSpec((2, tq, DIFF_DK), lambda b, h, p, qt, kt: (h, b * nq + kt[p], 0)),
            pl.BlockSpec((tq, DIFF_DV), lambda b, h, p, qt, kt: (b * nq + kt[p], h)),
            pl.BlockSpec((4, DIFF_DK), lambda b, h, p, qt, kt: (0, 0)),
            pl.BlockSpec((1, DIFF_DV), lambda b, h, p, qt, kt: (0, 0)),
        ],
        out_specs=pl.BlockSpec((tq, 2 * DIFF_DV), lambda b, h, p, qt, kt: (b * nq + qt[p], h)),
        scratch_shapes=[pltpu.VMEM((2, rows, LANES), F32), pltpu.VMEM((2, rows, LANES), F32),
                        pltpu.VMEM((2, rows, DIFF_DV), F32)],
    )
    return pl.pallas_call(
        functools.partial(_flash_kernel, mode='diff', tq=tq, lam_init=lam_init), grid_spec=grid_spec,
        out_shape=jax.ShapeDtypeStruct((nb * t, 4 * DIFF_DV), F32),
        compiler_params=_cparams("parallel", "parallel", "arbitrary"), name="prompt_diff_attn",
    )(qi_tab, ki_tab, qd, kh, vbf, lw['diff_lambda'], lw['diff_subln'])


def _prompt_mla_attn(qm, kvbf, nb, t, lw):
    tq = _tile(t, 512)
    nq = t // tq
    qi_tab, ki_tab = _pair_tables(nq)
    rows = MLA_HEADS * tq
    grid_spec = pltpu.PrefetchScalarGridSpec(
        num_scalar_prefetch=2, grid=(nb, len(qi_tab)),
        in_specs=[
            pl.BlockSpec((MLA_HEADS, tq, MLA_CACHE_W), lambda b, p, qt, kt: (0, b * nq + qt[p], 0)),
            pl.BlockSpec((tq, MLA_CACHE_W), lambda b, p, qt, kt: (b * nq + kt[p], 0)),
            pl.BlockSpec((MLA_HEADS, MLA_KV_LORA, MLA_V), lambda b, p, qt, kt: (0, 0, 0)),
        ],
        out_specs=pl.BlockSpec((tq, MLA_HEADS * MLA_V), lambda b, p, qt, kt: (b * nq + qt[p], 0)),
        scratch_shapes=[pltpu.VMEM((1, rows, LANES), F32), pltpu.VMEM((1, rows, LANES), F32),
                        pltpu.VMEM((1, rows, MLA_KV_LORA), F32)],
    )
    return pl.pallas_call(
        functools.partial(_flash_kernel, mode='mla', tq=tq, lam_init=0.0), grid_spec=grid_spec,
        out_shape=jax.ShapeDtypeStruct((nb * t, MLA_HEADS * MLA_V), F32),
        compiler_params=_cparams("parallel", "arbitrary"), name="prompt_mla_attn",
    )(qi_tab, ki_tab, qm, kvbf, lw['wuv'])


def _decode_kernel(pt_ref, *refs, mode, layer, npg, tn, lam_init):
    if mode == 'diff':
        (q_ref, ck_hbm, cv_hbm, kn_ref, vn_ref, lam_ref, g_ref, o_ref,
         kbuf, vbuf, sem, m_sc, l_sc, acc_sc) = refs
        groups = 4
    else:
        q_ref, ck_hbm, kn_ref, wuv_ref, o_ref, kbuf, sem, m_sc, l_sc, acc_sc = refs
        groups = 1
    b = pl.program_id(0)
    c = pl.program_id(1)
    nchunk = pl.num_programs(1)
    step = b * nchunk + c
    slot = step % 2
    rows = q_ref.shape[1]
    gr = rows // groups

    def page_copies(seq, chunk, sl):
        out = []
        for j in range(npg):
            page = 0 if seq is None else pt_ref[seq, chunk * npg + j]
            out.append(pltpu.make_async_copy(ck_hbm.at[layer, page], kbuf.at[sl, j], sem.at[0, sl, j]))
            if mode == 'diff':
                out.append(pltpu.make_async_copy(cv_hbm.at[layer, page], vbuf.at[sl, j], sem.at[1, sl, j]))
        return out

    @pl.when(step == 0)
    def _():
        for cp in page_copies(b, c, slot):
            cp.start()

    nxt = step + 1

    @pl.when(nxt < pl.num_programs(0) * nchunk)
    def _():
        for cp in page_copies(nxt // nchunk, nxt % nchunk, 1 - slot):
            cp.start()

    @pl.when(c == 0)
    def _():
        m_sc[...] = jnp.full(m_sc.shape, NEG_BIG, F32)
        l_sc[...] = jnp.zeros(l_sc.shape, F32)
        acc_sc[...] = jnp.zeros(acc_sc.shape, F32)

    for cp in page_copies(None, None, slot):
        cp.wait()

    q = q_ref[0]
    if mode == 'diff':
        parts = []
        for h in range(2):
            for m in range(2):
                kcat = jnp.concatenate([kbuf[slot, j, h, m].astype(BF16) for j in range(npg)], axis=1)
                g = h * 2 + m
                parts.append(_dot(q[g * gr:(g + 1) * gr], kcat))
        s = jnp.concatenate(parts, axis=0)
    else:
        kcat = jnp.concatenate([kbuf[slot, j].astype(BF16) for j in range(npg)], axis=1)
        s = _dot(q, kcat)
    m_prev = m_sc[...]
    m_new = jnp.maximum(m_prev, jnp.max(s, axis=-1, keepdims=True))
    alpha = jnp.exp(m_prev - m_new)
    p = jnp.exp(s - m_new).astype(BF16)
    l_sc[...] = alpha * l_sc[...] + jnp.sum(p.astype(F32), axis=-1, keepdims=True)
    if mode == 'diff':
        nrow = 2 * PAGE_SIZE
        ri = lax.broadcasted_iota(jnp.int32, (nrow, nrow), 0)
        ci = lax.broadcasted_iota(jnp.int32, (nrow, nrow), 1)
        sel = (ci == 2 * (ri % PAGE_SIZE) + ri // PAGE_SIZE).astype(BF16)
        split = [_dot(sel, vbuf[slot, j].astype(BF16)).astype(BF16) for j in range(npg)]
        pv = jnp.concatenate(
            [_dot(p[h * 2 * gr:(h + 1) * 2 * gr],
                  jnp.concatenate([sp[h * PAGE_SIZE:(h + 1) * PAGE_SIZE] for sp in split], axis=0))
             for h in range(2)], axis=0)
    else:
        pv = _dot_nt(p, kcat[:MLA_KV_LORA])
    acc_sc[...] = alpha * acc_sc[...] + pv
    m_sc[...] = m_new

    @pl.when(c == pl.num_programs(1) - 1)
    def _():
        qf = q.astype(F32)
        kn = kn_ref[0].astype(BF16).astype(F32)
        vn = (vn_ref[0] if mode == 'diff' else kn_ref[0][:, :MLA_KV_LORA]).astype(BF16).astype(F32)
        tok = lax.broadcasted_iota(jnp.int32, (rows, 1), 0) % tn
        sn = []
        for j in range(tn):
            if mode == 'diff':
                dots = [jnp.sum(qf[g * gr:(g + 1) * gr] * kn[j:j + 1, g * DIFF_DK:(g + 1) * DIFF_DK],
                                axis=-1, keepdims=True) for g in range(groups)]
                sj = jnp.concatenate(dots, axis=0)
            else:
                sj = jnp.sum(qf * kn[j:j + 1], axis=-1, keepdims=True)
            sn.append(jnp.where(tok >= j, sj, NEG_BIG))
        m_prev = m_sc[...]
        m_new = m_prev
        for sj in sn:
            m_new = jnp.maximum(m_new, sj)
        alpha = jnp.exp(m_prev - m_new)
        l = alpha * l_sc[...]
        acc = alpha * acc_sc[...]
        for j, sj in enumerate(sn):
            pj = jnp.exp(sj - m_new)
            l = l + pj
            if mode == 'diff':
                vrow = jnp.concatenate(
                    [jnp.broadcast_to(vn[j:j + 1, h * DIFF_DV:(h + 1) * DIFF_DV], (2 * gr, DIFF_DV)) for h in range(2)],
                    axis=0)
            else:
                vrow = vn[j:j + 1]
            acc = acc + pj * vrow
        o = acc / l
        if mode == 'diff':
            lam = _diff_lambda(lam_ref, lam_init)
            for h in range(2):
                base = h * 2 * gr
                oh = o[base:base + gr] - lam * o[base + gr:base + 2 * gr]
                y = _rms(oh, g_ref[...], SUBLN_EPS) * (1.0 - lam_init)
                for r in range(2):
                    col = (h * 2 + r) * DIFF_DV
                    o_ref[0, :, col:col + DIFF_DV] = y[r * tn:(r + 1) * tn]
        else:
            ob = o.astype(BF16)
            for h in range(MLA_HEADS):
                o_ref[0, :, h * MLA_V:(h + 1) * MLA_V] = _dot(ob[h * tn:(h + 1) * tn], wuv_ref[h])


def _sample_attn(mode, layer, q, cache_k, cache_v, knew, vnew, page_table, lw, lam_init):
    nb, rows, dk = q.shape
    n_pages = page_table.shape[1]
    npg = min(PAGES_PER_STEP, n_pages)
    assert n_pages % npg == 0
    tn = knew.shape[1]

    seq3 = lambda b, c, pt: (b, 0, 0)
    hbm = pl.BlockSpec(memory_space=pl.ANY)
    page_buf = lambda cache: pltpu.VMEM((2, npg) + cache.shape[2:], F32)
    in_specs = [pl.BlockSpec((1, rows, dk), seq3), hbm]
    operands = [q, cache_k]
    scratch = [page_buf(cache_k)]
    if mode == 'diff':
        in_specs += [hbm, pl.BlockSpec((1, tn, knew.shape[-1]), seq3), pl.BlockSpec((1, tn, vnew.shape[-1]), seq3),
                     pl.BlockSpec((4, DIFF_DK), lambda b, c, pt: (0, 0)),
                     pl.BlockSpec((1, DIFF_DV), lambda b, c, pt: (0, 0))]
        operands += [cache_v, knew, vnew, lw['diff_lambda'], lw['diff_subln']]
        scratch += [page_buf(cache_v)]
        dv, width = DIFF_DV, 4 * DIFF_DV
    else:
        in_specs += [pl.BlockSpec((1, tn, knew.shape[-1]), seq3),
                     pl.BlockSpec((MLA_HEADS, MLA_KV_LORA, MLA_V), lambda b, c, pt: (0, 0, 0))]
        operands += [knew, lw['wuv']]
        dv, width = MLA_KV_LORA, MLA_HEADS * MLA_V
    scratch += [pltpu.SemaphoreType.DMA((2, 2, npg)),
                pltpu.VMEM((rows, 1), F32), pltpu.VMEM((rows, 1), F32), pltpu.VMEM((rows, dv), F32)]
    grid_spec = pltpu.PrefetchScalarGridSpec(
        num_scalar_prefetch=1, grid=(nb, n_pages // npg), in_specs=in_specs,
        out_specs=pl.BlockSpec((1, tn, width), seq3), scratch_shapes=scratch,
    )
    return pl.pallas_call(
        functools.partial(_decode_kernel, mode=mode, layer=layer, npg=npg, tn=tn, lam_init=lam_init),
        grid_spec=grid_spec, out_shape=jax.ShapeDtypeStruct((nb, tn, width), F32),
        compiler_params=_cparams("arbitrary", "arbitrary"), name="sample_%s_attn" % mode,
    )(page_table, *operands)


def _mix_out_kernel(x_ref, yrw_ref, yd_ref, ym_ref, w1, w2, w3, gx_ref, wxq, h_o, qx_o):
    h = (x_ref[...] + _dot(yrw_ref[...].astype(BF16), w1[...]) + _dot(yd_ref[...].astype(BF16), w2[...])
         + _dot(ym_ref[...].astype(BF16), w3[...]))
    h_o[...] = h
    qx_o[...] = _dot(_rms(h, gx_ref[...], RMS_EPS).astype(BF16), wxq[...])


def _mix_out(x, yrw, yd, ym, lw):
    n = x.shape[0]
    tm = _tile(n, 512)
    row = lambda w: pl.BlockSpec((tm, w), lambda i: (i, 0))
    ws = [lw['w_out_rw'], lw['w_out_d'], lw['w_out_m'], lw['g_cross'], lw['w_xq']]
    return pl.pallas_call(
        _mix_out_kernel, grid=(n // tm,),
        in_specs=[row(D_MODEL), row(RWKV_W), row(4 * DIFF_DV), row(MLA_HEADS * MLA_V)] + [_full(w.shape) for w in ws],
        out_specs=[row(D_MODEL), row(MEM_W)],
        out_shape=[jax.ShapeDtypeStruct((n, D_MODEL), F32), jax.ShapeDtypeStruct((n, MEM_W), F32)],
        compiler_params=_cparams("parallel"), name="mix_out",
    )(x, yrw, yd, ym, *ws)


def _mem_kv_kernel(x_ref, g_ref, wk, wv, k_o, v_o):
    n = _rms(x_ref[...], g_ref[...], RMS_EPS).astype(BF16)
    k_o[...] = _dot(n, wk[...])
    v_o[...] = _dot(n, wv[...])


def _mem_kv(mem, lw):
    n = mem.shape[0]
    tm = _tile(n, 512)
    row = lambda w: pl.BlockSpec((tm, w), lambda i: (i, 0))
    return pl.pallas_call(
        _mem_kv_kernel, grid=(n // tm,),
        in_specs=[row(D_MODEL), _full((1, D_MODEL)), _full(lw['w_xk'].shape), _full(lw['w_xv'].shape)],
        out_specs=[row(MEM_W), row(MEM_W)],
        out_shape=[jax.ShapeDtypeStruct((n, MEM_W), F32)] * 2,
        compiler_params=_cparams("parallel"), name="mem_kv",
    )(mem, lw['g_mem'], lw['w_xk'], lw['w_xv'])


def _cross_kernel(q_ref, k_ref, v_ref, o_ref):
    q = q_ref[...].astype(BF16)
    k = k_ref[0, 0].astype(BF16)
    v = v_ref[0, 0].astype(BF16)
    for h in range(MEM_HEADS):
        sl = slice(h * MEM_HEAD_DIM, (h + 1) * MEM_HEAD_DIM)
        s = _dot_nt(q[:, sl], k[:, sl])
        p = jnp.exp(s - jnp.max(s, axis=-1, keepdims=True))
        p = p / jnp.sum(p, axis=-1, keepdims=True)
        o_ref[:, sl] = _dot(p.astype(BF16), v[:, sl])


def _cross_attn(qx, row0, nseq, t, mk, mv, layer):
    tq = _tile(t, 512)
    nt = t // tq
    blk0 = row0 // tq
    mem = pl.BlockSpec((1, 1) + mk.shape[2:], lambda s, i: (layer, s, 0, 0))
    return pl.pallas_call(
        _cross_kernel, grid=(nseq, nt),
        in_specs=[pl.BlockSpec((tq, MEM_W), lambda s, i: (blk0 + s * nt + i, 0)), mem, mem],
        out_specs=pl.BlockSpec((tq, MEM_W), lambda s, i: (s * nt + i, 0)),
        out_shape=jax.ShapeDtypeStruct((nseq * t, MEM_W), F32),
        compiler_params=_cparams("parallel", "arbitrary"), name="cross_attn",
    )(qx, mk, mv)


def _ffn_kernel(h_ref, ox_ref, wxo, g_ref, wg, wu, wd, o_ref, h2_sc, n_sc, acc_sc):
    f = pl.program_id(1)

    @pl.when(f == 0)
    def _():
        h2 = h_ref[...] + _dot(ox_ref[...].astype(BF16), wxo[...])
        h2_sc[...] = h2
        n_sc[...] = _rms(h2, g_ref[...], RMS_EPS).astype(BF16)
        acc_sc[...] = jnp.zeros(acc_sc.shape, F32)

    n = n_sc[...]
    hg = _dot(n, wg[...])
    acc_sc[...] += _dot((jax.nn.silu(hg) * _dot(n, wu[...])).astype(BF16), wd[...])

    @pl.when(f == pl.num_programs(1) - 1)
    def _():
        o_ref[...] = h2_sc[...] + acc_sc[...]


def _ffn_dense(h, ox, lw):
    n = h.shape[0]
    tm = _tile(n, 1024)
    dff = lw['ffn_wg'].shape[1]
    tf = 256
    assert dff % tf == 0
    row = lambda w: pl.BlockSpec((tm, w), lambda i, f: (i, 0))
    return pl.pallas_call(
        _ffn_kernel, grid=(n // tm, dff // tf),
        in_specs=[row(D_MODEL), row(MEM_W), pl.BlockSpec((MEM_W, D_MODEL), lambda i, f: (0, 0)),
                  pl.BlockSpec((1, D_MODEL), lambda i, f: (0, 0)),
                  pl.BlockSpec((D_MODEL, tf), lambda i, f: (0, f)), pl.BlockSpec((D_MODEL, tf), lambda i, f: (0, f)),
                  pl.BlockSpec((tf, D_MODEL), lambda i, f: (f, 0))],
        out_specs=row(D_MODEL), out_shape=jax.ShapeDtypeStruct((n, D_MODEL), F32),
        scratch_shapes=[pltpu.VMEM((tm, D_MODEL), F32), pltpu.VMEM((tm, D_MODEL), BF16),
                        pltpu.VMEM((tm, D_MODEL), F32)],
        compiler_params=_cparams("parallel", "arbitrary"), name="ffn_dense",
    )(h, ox, lw['w_xo'], lw['g_ffn'], lw['ffn_wg'], lw['ffn_wu'], lw['ffn_wd'])


def _router_kernel(h_ref, ox_ref, wxo, g_ref, wr, h2_o, n_o, comb_o, pos_o, cnt_o):
    h2 = h_ref[...] + _dot(ox_ref[...].astype(BF16), wxo[...])
    h2_o[...] = h2
    nf = _rms(h2, g_ref[...], RMS_EPS)
    n_o[...] = nf.astype(BF16)
    logits = _dot_hi(nf, wr[...])
    tm = logits.shape[0]
    eid = lax.broadcasted_iota(jnp.int32, logits.shape, 1)
    m1 = jnp.max(logits, axis=-1, keepdims=True)
    i1 = jnp.min(jnp.where(logits == m1, eid, N_EXPERTS), axis=-1, keepdims=True)
    sel1 = eid == i1
    rest = jnp.where(sel1, -jnp.inf, logits)
    m2 = jnp.max(rest, axis=-1, keepdims=True)
    i2 = jnp.min(jnp.where(rest == m2, eid, N_EXPERTS), axis=-1, keepdims=True)
    sel2 = eid == i2
    e2 = jnp.exp(m2 - m1)
    den = 1.0 + e2
    comb_o[...] = jnp.where(sel1, 1.0 / den, 0.0) + jnp.where(sel2, e2 / den, 0.0)
    sel = jnp.logical_or(sel1, sel2).astype(F32)
    lower = (lax.broadcasted_iota(jnp.int32, (tm, tm), 1) < lax.broadcasted_iota(jnp.int32, (tm, tm), 0))
    rank = _dot(lower.astype(BF16), sel.astype(BF16))
    pos_o[...] = jnp.where(sel > 0.0, rank, -1.0)
    cnt_o[0] = jnp.sum(sel, axis=0, keepdims=True)


def _router(h, ox, lw, tm):
    n = h.shape[0]
    row = lambda w: pl.BlockSpec((tm, w), lambda i: (i, 0))
    return pl.pallas_call(
        _router_kernel, grid=(n // tm,),
        in_specs=[row(D_MODEL), row(MEM_W), _full((MEM_W, D_MODEL)), _full((1, D_MODEL)),
                  _full((D_MODEL, N_EXPERTS))],
        out_specs=[row(D_MODEL), row(D_MODEL), row(N_EXPERTS), row(N_EXPERTS),
                   pl.BlockSpec((1, 1, N_EXPERTS), lambda i: (i, 0, 0))],
        out_shape=[jax.ShapeDtypeStruct((n, D_MODEL), F32), jax.ShapeDtypeStruct((n, D_MODEL), BF16),
                   jax.ShapeDtypeStruct((n, N_EXPERTS), F32), jax.ShapeDtypeStruct((n, N_EXPERTS), F32),
                   jax.ShapeDtypeStruct((n // tm, 1, N_EXPERTS), F32)],
        compiler_params=_cparams("parallel"), name="moe_router",
    )(h, ox, lw['w_xo'], lw['g_ffn'], lw['moe_router'])


def _moe_kernel(cnt_ref, n_ref, post_ref, pos_ref, comb_ref, h2_ref, wg, wu, wd, o_ref, xe_sc, acc_sc):
    i = pl.program_id(0)
    e = pl.program_id(1)
    f = pl.program_id(2)
    tm = n_ref.shape[0]
    ch = MOE_CHUNK
    nch = (cnt_ref[i * N_EXPERTS + e] + ch - 1) // ch

    @pl.when(jnp.logical_and(e == 0, f == 0))
    def _():
        o_ref[...] = h2_ref[...]

    @pl.when(f == 0)
    def _():
        esub = lax.broadcasted_iota(jnp.int32, (N_EXPERTS, tm), 0)
        pos_row = jnp.sum(jnp.where(esub == e, post_ref[...], 0.0), axis=0, keepdims=True)
        slot = lax.broadcasted_iota(jnp.int32, (ch, tm), 0).astype(F32)

        def gather(c, carry):
            base = pl.multiple_of(c * ch, ch)
            onehot = (pos_row == slot + (c * ch).astype(F32)).astype(BF16)
            xe_sc[pl.ds(base, ch), :] = _dot(onehot, n_ref[...]).astype(BF16)
            acc_sc[pl.ds(base, ch), :] = jnp.zeros((ch, D_MODEL), F32)
            return carry

        lax.fori_loop(0, nch, gather, 0)

    def expert(c, carry):
        base = pl.multiple_of(c * ch, ch)
        x = xe_sc[pl.ds(base, ch), :]
        hh = (jax.nn.silu(_dot(x, wg[0])) * _dot(x, wu[0])).astype(BF16)
        acc_sc[pl.ds(base, ch), :] += _dot(hh, wd[0])
        return carry

    lax.fori_loop(0, nch, expert, 0)

    @pl.when(f == pl.num_programs(2) - 1)
    def _():
        elane = lax.broadcasted_iota(jnp.int32, (tm, N_EXPERTS), 1)
        pos_col = jnp.sum(jnp.where(elane == e, pos_ref[...], 0.0), axis=-1, keepdims=True)
        gate_col = jnp.sum(jnp.where(elane == e, comb_ref[...], 0.0), axis=-1, keepdims=True)
        slot = lax.broadcasted_iota(jnp.int32, (tm, ch), 1).astype(F32)

        def scatter(c, carry):
            base = pl.multiple_of(c * ch, ch)
            onehot = (pos_col == slot + (c * ch).astype(F32)).astype(BF16)
            y = acc_sc[pl.ds(base, ch), :].astype(BF16)
            o_ref[...] += gate_col * _dot(onehot, y)
            return carry

        lax.fori_loop(0, nch, scatter, 0)


def _moe(h, ox, lw):
    n = h.shape[0]
    tm = _tile(n, 1024)
    h2, nb, comb, pos, cnt = _router(h, ox, lw, tm)
    pos_t = pos.T
    cnt = cnt.reshape(-1).astype(jnp.int32)
    dff = lw['moe_wg'].shape[2]
    tf = 1792 if dff % 1792 == 0 else 512
    assert dff % tf == 0
    tok = lambda w: pl.BlockSpec((tm, w), lambda i, e, f, c: (i, 0))
    grid_spec = pltpu.PrefetchScalarGridSpec(
        num_scalar_prefetch=1, grid=(n // tm, N_EXPERTS, dff // tf),
        in_specs=[tok(D_MODEL), pl.BlockSpec((N_EXPERTS, tm), lambda i, e, f, c: (0, i)),
                  tok(N_EXPERTS), tok(N_EXPERTS), tok(D_MODEL),
                  pl.BlockSpec((1, D_MODEL, tf), lambda i, e, f, c: (e, 0, f)),
                  pl.BlockSpec((1, D_MODEL, tf), lambda i, e, f, c: (e, 0, f)),
                  pl.BlockSpec((1, tf, D_MODEL), lambda i, e, f, c: (e, f, 0))],
        out_specs=tok(D_MODEL),
        scratch_shapes=[pltpu.VMEM((tm, D_MODEL), BF16), pltpu.VMEM((tm, D_MODEL), F32)],
    )
    return pl.pallas_call(
        _moe_kernel, grid_spec=grid_spec, out_shape=jax.ShapeDtypeStruct((n, D_MODEL), F32),
        compiler_params=_cparams("parallel", "arbitrary", "arbitrary"), name="moe_experts",
    )(cnt, nb, pos_t, pos, comb, h2, lw['moe_wg'], lw['moe_wu'], lw['moe_wd'])


def _final_norm_kernel(x_ref, g_ref, o_ref):
    o_ref[...] = _rms(x_ref[...], g_ref[...], RMS_EPS)


def _final_norm(x, row0, rows, g):
    tm = _tile(rows, 1024)
    blk0 = row0 // tm
    return pl.pallas_call(
        _final_norm_kernel, grid=(rows // tm,),
        in_specs=[pl.BlockSpec((tm, D_MODEL), lambda i: (blk0 + i, 0)), _full((1, D_MODEL))],
        out_specs=pl.BlockSpec((tm, D_MODEL), lambda i: (i, 0)),
        out_shape=jax.ShapeDtypeStruct((rows, D_MODEL), F32),
        compiler_params=_cparams("parallel"), name="final_norm",
    )(x, g)


def _rot_cols(w):
    half = w.shape[-1] // 2
    return jnp.concatenate([-w[..., half:], w[..., :half]], axis=-1)


def _layer_weights(p, l):
    bf = lambda a: a.astype(BF16)
    r2 = lambda a: a.reshape(1, -1).astype(F32)
    w_in = p['w_in'][l]
    c1 = RWKV_IN
    c2 = c1 + DIFF_Q_W
    c3 = c2 + DIFF_K_W
    c4 = c3 + DIFF_V_W
    c5 = c4 + MLA_Q_LORA
    c6 = c5 + MLA_KV_LORA
    wq = w_in[:, c1:c2].reshape(D_MODEL, 2, 2, 2, DIFF_DK).transpose(0, 1, 3, 2, 4).reshape(D_MODEL, DIFF_Q_W)
    wuq = p['mla_wuq'][l]
    w_out = p['w_out'][l]
    lw = dict(
        g_mix=r2(p['norm_mix'][l]), w_rw=bf(w_in[:, :c1]), w_q=bf(wq * DIFF_DK ** -0.5), w_k=bf(w_in[:, c2:c3]),
        w_v=bf(w_in[:, c3:c4]), w_cq=bf(w_in[:, c4:c5]), w_ckv=bf(w_in[:, c5:c6]), w_kr=bf(w_in[:, c6:]),
        w_krr=bf(_rot_cols(w_in[:, c6:])), g_q=r2(p['mla_qnorm'][l]), g_kv=r2(p['mla_kvnorm'][l]),
        wq_nope=bf(wuq[:, :, :MLA_NOPE].transpose(1, 0, 2)), wq_rope=bf(wuq[:, :, MLA_NOPE:].transpose(1, 0, 2)),
        wq_rope_rot=bf(_rot_cols(wuq[:, :, MLA_NOPE:]).transpose(1, 0, 2)),
        wuk_t=bf(p['mla_wuk'][l].transpose(1, 2, 0)), wuv=bf(p['mla_wuv'][l].transpose(1, 0, 2)),
        rw_mu=r2(p['rwkv_mu'][l]), rw_w0=r2(p['rwkv_w0'][l]), rw_a0=r2(p['rwkv_a0'][l]), rw_kk=r2(p['rwkv_kk'][l]),
        rw_ka=r2(p['rwkv_ka'][l]), rw_rk=r2(p['rwkv_rk'][l]), rw_w2=bf(p['rwkv_w2'][l]), rw_a2=bf(p['rwkv_a2'][l]),
        rw_g2=bf(p['rwkv_g2'][l]), rw_lnw=r2(p['rwkv_lnw'][l]), rw_lnb=r2(p['rwkv_lnb'][l]),
        bones=jnp.kron(jnp.eye(RWKV_HEADS, dtype=F32), jnp.ones((RWKV_N, RWKV_N), F32)),
        diff_lambda=p['diff_lambda'][l].astype(F32), diff_subln=r2(p['diff_subln'][l]),
        w_out_rw=bf(w_out[:RWKV_W]), w_out_d=bf(w_out[RWKV_W:RWKV_W + 4 * DIFF_DV]),
        w_out_m=bf(w_out[RWKV_W + 4 * DIFF_DV:]), g_cross=r2(p['norm_cross'][l]),
        w_xq=bf(p['w_xq'][l] * MEM_HEAD_DIM ** -0.5), g_mem=r2(p['norm_mem'][l]), w_xk=bf(p['w_xk'][l]),
        w_xv=bf(p['w_xv'][l]), w_xo=bf(p['w_xo'][l]), g_ffn=r2(p['norm_ffn'][l]),
    )
    i = l // 2
    if l % 2 == 0:
        lw.update(ffn_wg=bf(p['ffn_wg'][i]), ffn_wu=bf(p['ffn_wu'][i]), ffn_wd=bf(p['ffn_wd'][i]))
    else:
        lw.update(moe_router=p['moe_router'][i].astype(F32), moe_wg=bf(p['moe_wg'][i]), moe_wu=bf(p['moe_wu'][i]),
                  moe_wd=bf(p['moe_wd'][i]))
    return lw


def _rope_tables(pos):
    half = MLA_ROPE // 2
    inv = ROPE_THETA ** (-jnp.arange(half, dtype=F32) / half)
    ang = pos.astype(F32)[:, None] * inv[None, :]
    cos, sin = jnp.cos(ang), jnp.sin(ang)
    return jnp.concatenate([cos, cos], axis=-1), jnp.concatenate([sin, sin], axis=-1)


def kernel(x_prompt, x_sample, state_rwkv, state_shift, cache_diff_k, cache_diff_v, cache_mla_kv, cache_mem_k, cache_mem_v, page_table, mem_prompt, norm_mix, w_in, rwkv_mu, rwkv_w0, rwkv_w2, rwkv_a0, rwkv_a2, rwkv_g2, rwkv_kk, rwkv_ka, rwkv_rk, rwkv_lnw, rwkv_lnb, diff_lambda, diff_subln, mla_qnorm, mla_kvnorm, mla_wuq, mla_wuk, mla_wuv, w_out, norm_cross, norm_mem, w_xq, w_xk, w_xv, w_xo, norm_ffn, ffn_wg, ffn_wu, ffn_wd, moe_router, moe_wg, moe_wu, moe_wd, final_norm):
    p = dict(norm_mix=norm_mix, w_in=w_in, rwkv_mu=rwkv_mu, rwkv_w0=rwkv_w0, rwkv_w2=rwkv_w2, rwkv_a0=rwkv_a0,
             rwkv_a2=rwkv_a2, rwkv_g2=rwkv_g2, rwkv_kk=rwkv_kk, rwkv_ka=rwkv_ka, rwkv_rk=rwkv_rk,
             rwkv_lnw=rwkv_lnw, rwkv_lnb=rwkv_lnb, diff_lambda=diff_lambda, diff_subln=diff_subln,
             mla_qnorm=mla_qnorm, mla_kvnorm=mla_kvnorm, mla_wuq=mla_wuq, mla_wuk=mla_wuk, mla_wuv=mla_wuv,
             w_out=w_out, norm_cross=norm_cross, norm_mem=norm_mem, w_xq=w_xq, w_xk=w_xk, w_xv=w_xv, w_xo=w_xo,
             norm_ffn=norm_ffn, ffn_wg=ffn_wg, ffn_wu=ffn_wu, ffn_wd=ffn_wd, moe_router=moe_router, moe_wg=moe_wg,
             moe_wu=moe_wu, moe_wd=moe_wd)
    bp, t = x_prompt.shape[:2]
    bs, tn = x_sample.shape[:2]
    depth = w_in.shape[0]
    n_p, n_s = bp * t, bs * tn
    past = page_table.shape[1] * PAGE_SIZE
    mem_len = mem_prompt.shape[1]

    x = jnp.concatenate([x_prompt.reshape(n_p, D_MODEL), x_sample.reshape(n_s, D_MODEL)], axis=0)
    pos = jnp.concatenate([jnp.tile(jnp.arange(t), bp), jnp.tile(past + jnp.arange(tn), bs)])
    cos, sin = _rope_tables(pos)
    ck_t = cache_diff_k.transpose(0, 1, 3, 4, 5, 2)
    ckv_t = cache_mla_kv.transpose(0, 1, 3, 2)
    cv_rows = cache_diff_v.reshape(depth, cache_diff_v.shape[1], 2 * PAGE_SIZE, DIFF_DV)
    cmk = cache_mem_k.reshape(depth, bs, mem_len, MEM_W)
    cmv = cache_mem_v.reshape(depth, bs, mem_len, MEM_W)
    mem = mem_prompt.reshape(bp * mem_len, D_MODEL)
    zeros_shift = jnp.zeros((bp, RWKV_IN), F32)
    zeros_state = jnp.zeros((bp, RWKV_HEADS, RWKV_N, RWKV_N), F32)

    outs = {k: [] for k in ('pS', 'psh', 'pdk', 'pdv', 'pkv', 'pmk', 'pmv', 'sS', 'ssh', 'sdk', 'sdv', 'skv')}
    for l in range(depth):
        lw = _layer_weights(p, l)
        lam_init = 0.8 - 0.6 * math.exp(-0.3 * l)
        xrw, qd, ktok, kh, vtok, vbf, kv, kvbf, qm = _mixer_in(x, lw, cos, sin)

        yrw_p, st_p = _rwkv_mix(_rwkv_prep(xrw, 0, bp, t, zeros_shift, lw), zeros_state, bp, t, lw)
        yrw_s, st_s = _rwkv_mix(_rwkv_prep(xrw, n_p, bs, tn, state_shift[l], lw), state_rwkv[l], bs, tn, lw)

        yd_p = _prompt_diff_attn(qd, kh, vbf, bp, t, lw, lam_init)
        qd_s = qd[:, n_p:].reshape(8, bs, tn, DIFF_DK).transpose(1, 0, 2, 3).reshape(bs, 8 * tn, DIFF_DK)
        yd_s = _sample_attn('diff', l, qd_s, ck_t, cv_rows,
                            ktok[n_p:].reshape(bs, tn, DIFF_K_W), vtok[n_p:].reshape(bs, tn, DIFF_V_W),
                            page_table, lw, lam_init)

        ym_p = _prompt_mla_attn(qm, kvbf, bp, t, lw)
        qm_s = qm[:, n_p:].reshape(MLA_HEADS, bs, tn, MLA_CACHE_W).transpose(1, 0, 2, 3)
        ym_s = _sample_attn('mla', l, qm_s.reshape(bs, MLA_HEADS * tn, MLA_CACHE_W), ckv_t, None,
                            kv[n_p:].reshape(bs, tn, MLA_CACHE_W), None, page_table, lw, lam_init)

        cat = lambda a, b: jnp.concatenate([a, b.reshape(n_s, -1)], axis=0)
        h, qx = _mix_out(x, cat(yrw_p, yrw_s), cat(yd_p, yd_s), cat(ym_p, ym_s), lw)

        mk, mv = _mem_kv(mem, lw)
        ox_p = _cross_attn(qx, 0, bp, t, mk.reshape(1, bp, mem_len, MEM_W), mv.reshape(1, bp, mem_len, MEM_W), 0)
        ox_s = _cross_attn(qx, n_p, bs, tn, cmk, cmv, l)
        ox = jnp.concatenate([ox_p, ox_s], axis=0)

        x = _ffn_dense(h, ox, lw) if l % 2 == 0 else _moe(h, ox, lw)

        outs['pS'].append(st_p)
        outs['psh'].append(xrw[:n_p].reshape(bp, t, RWKV_IN)[:, -1])
        outs['pdk'].append(ktok[:n_p].reshape(bp, t, 2, 2, DIFF_DK))
        outs['pdv'].append(vtok[:n_p].reshape(bp, t, 2, DIFF_DV))
        outs['pkv'].append(kv[:n_p].reshape(bp, t, MLA_CACHE_W))
        outs['pmk'].append(mk.reshape(bp, mem_len, MEM_HEADS, MEM_HEAD_DIM))
        outs['pmv'].append(mv.reshape(bp, mem_len, MEM_HEADS, MEM_HEAD_DIM))
        outs['sS'].append(st_s)
        outs['ssh'].append(xrw[n_p:].reshape(bs, tn, RWKV_IN)[:, -1])
        outs['sdk'].append(ktok[n_p:].reshape(bs, tn, 2, 2, DIFF_DK))
        outs['sdv'].append(vtok[n_p:].reshape(bs, tn, 2, DIFF_DV))
        outs['skv'].append(kv[n_p:].reshape(bs, tn, MLA_CACHE_W))

    g = final_norm.reshape(1, D_MODEL).astype(F32)
    y_prompt = _final_norm(x, 0, n_p, g).reshape(bp, t, D_MODEL)
    y_sample = _final_norm(x, n_p, n_s, g).reshape(bs, tn, D_MODEL)
    st = lambda k: jnp.stack(outs[k])
    return (y_prompt, y_sample, st('pS'), st('psh'), st('pdk'), st('pdv'), st('pkv'), st('pmk'), st('pmv'),
            st('sS'), st('ssh'), st('sdk'), st('sdv'), st('skv'))
```
